```python
import jax, jax.numpy as jnp
from jax import lax
import numpy as np

D_MODEL = 1024
BATCH = 8
SEQ = 2048
DEPTH = 2

HEAD_DIM = 64
NA_HEADS = 8
GQA_Q_HEADS = 8
GQA_KV_HEADS = 2
GRID_W = 64
NA_WIN_ROWS = 8
NA_WIN_COLS = 16
SWA_WINDOW = 128
SWA_BLOCK = 128
ROPE_THETA = 10000.0
PEER_HEADS = 8
PEER_N_KEYS = 128
PEER_N_EXPERTS = PEER_N_KEYS * PEER_N_KEYS
PEER_TOPK = 16
PEER_KEY_DIM = 256
PEER_CHUNK = 128
N_BRANCH = 2
EPS = 1e-6

NA_WIDTH = NA_HEADS * HEAD_DIM
GQA_Q_WIDTH = GQA_Q_HEADS * HEAD_DIM
GQA_KV_WIDTH = GQA_KV_HEADS * HEAD_DIM
IN_SPLITS = (NA_WIDTH, 2 * NA_WIDTH, 3 * NA_WIDTH,
             3 * NA_WIDTH + GQA_Q_WIDTH,
             3 * NA_WIDTH + GQA_Q_WIDTH + GQA_KV_WIDTH,
             3 * NA_WIDTH + GQA_Q_WIDTH + 2 * GQA_KV_WIDTH)
IN_COLS = IN_SPLITS[-1] + N_BRANCH * D_MODEL

kernel_name = "hybrid_natten_swa_peer_encoder"


def rms_norm(x, g):
    xf = x.astype(jnp.float32)
    y = xf * lax.rsqrt(jnp.mean(xf * xf, axis=-1, keepdims=True) + EPS)
    return (y * g.astype(jnp.float32)).astype(x.dtype)


def rotary(x, pos):
    half = x.shape[-1] // 2
    inv = ROPE_THETA ** (-jnp.arange(half, dtype=jnp.float32) / half)
    ang = pos.astype(jnp.float32)[:, None] * inv[None, :]
    cos = jnp.cos(ang)[None, :, None, :]
    sin = jnp.sin(ang)[None, :, None, :]
    xf = x.astype(jnp.float32)
    x1, x2 = xf[..., :half], xf[..., half:]
    out = jnp.concatenate([x1 * cos - x2 * sin, x1 * sin + x2 * cos], axis=-1)
    return out.astype(x.dtype)


def neighbourhood_attention(q, k, v, rpb):
    B, S, H, dh = q.shape
    rows = S // GRID_W
    kr = min(NA_WIN_ROWS, rows)
    r = jnp.arange(rows)
    rs = jnp.clip(r - kr // 2, 0, rows - kr)
    key_rows = rs[:, None] + jnp.arange(kr)[None, :]
    c = jnp.arange(GRID_W)
    cs = jnp.clip(c - NA_WIN_COLS // 2, 0, GRID_W - NA_WIN_COLS)
    qg = q.reshape(B, rows, GRID_W, H, dh)
    kg = k.reshape(B, rows, GRID_W, H, dh)[:, key_rows]
    vg = v.reshape(B, rows, GRID_W, H, dh)[:, key_rows]
    s = jnp.einsum('brqhd,brakhd->brhqak', qg, kg).astype(jnp.float32) * (dh ** -0.5)
    dr_idx = key_rows - r[:, None] + (NA_WIN_ROWS - 1)
    dc = c[None, :] - c[:, None]
    dc_idx = jnp.clip(dc + NA_WIN_COLS - 1, 0, 2 * NA_WIN_COLS - 2)
    in_win = (c[None, :] >= cs[:, None]) & (c[None, :] < cs[:, None] + NA_WIN_COLS)
    bias = rpb.astype(jnp.float32)[:, dr_idx[:, None, :, None], dc_idx[None, :, None, :]]
    bias = jnp.where(in_win[:, None, :], jnp.transpose(bias, (1, 0, 2, 3, 4)), -jnp.inf)
    p = jax.nn.softmax(s + bias[None], axis=(-2, -1))
    o = jnp.einsum('brhqak,brakhd->brqhd', p.astype(v.dtype), vg)
    return o.reshape(B, S, H * dh)


def sliding_window_gqa(q, k, v, sink):
    B, S, Hq, dh = q.shape
    Hkv = k.shape[2]
    G = Hq // Hkv
    nb = S // SWA_BLOCK
    pad = ((0, 0), (SWA_BLOCK, SWA_BLOCK), (0, 0), (0, 0))
    kp = jnp.pad(k, pad).reshape(B, nb + 2, SWA_BLOCK, Hkv, dh)
    vp = jnp.pad(v, pad).reshape(B, nb + 2, SWA_BLOCK, Hkv, dh)
    kw = jnp.concatenate([kp[:, :-2], kp[:, 1:-1], kp[:, 2:]], axis=2)
    vw = jnp.concatenate([vp[:, :-2], vp[:, 1:-1], vp[:, 2:]], axis=2)
    qb = q.reshape(B, nb, SWA_BLOCK, Hkv, G, dh)
    s = jnp.einsum('bnqkgd,bnckd->bnkgqc', qb, kw).astype(jnp.float32) * (dh ** -0.5)
    blk = jnp.arange(nb)[:, None]
    qpos = blk * SWA_BLOCK + jnp.arange(SWA_BLOCK)[None, :]
    kpos = (blk - 1) * SWA_BLOCK + jnp.arange(3 * SWA_BLOCK)[None, :]
    valid = ((kpos >= 0) & (kpos < S))[:, None, :] & \
        (jnp.abs(qpos[:, :, None] - kpos[:, None, :]) <= SWA_WINDOW)
    s = jnp.where(valid[None, :, None, None], s, -jnp.inf)
    sink_l = jnp.broadcast_to(sink.astype(jnp.float32).reshape(Hkv, G)[None, None, :, :, None, None],
                              s.shape[:-1] + (1,))
    p = jax.nn.softmax(jnp.concatenate([s, sink_l], axis=-1), axis=-1)[..., :-1]
    o = jnp.einsum('bnkgqc,bnckd->bnqkgd', p.astype(v.dtype), vw)
    return o.reshape(B, S, Hq * dh)


def token_mixer(h, w_in, gate_bias, qk_norm, na_rpb, swa_sink, w_branch_na, w_branch_swa, w_out, pos):
    B, S, _ = h.shape
    proj = h @ w_in
    qa, ka, va, qb, kb, vb, gl = jnp.split(proj, IN_SPLITS, axis=-1)
    heads = lambda t, n: t.reshape(B, S, n, HEAD_DIM)
    qa = rms_norm(heads(qa, NA_HEADS), qk_norm[0])
    ka = rms_norm(heads(ka, NA_HEADS), qk_norm[1])
    va = heads(va, NA_HEADS)
    qb = rotary(rms_norm(heads(qb, GQA_Q_HEADS), qk_norm[2]), pos)
    kb = rotary(rms_norm(heads(kb, GQA_KV_HEADS), qk_norm[3]), pos)
    vb = heads(vb, GQA_KV_HEADS)
    oa = neighbourhood_attention(qa, ka, va, na_rpb)
    ob = sliding_window_gqa(qb, kb, vb, swa_sink)
    gates = jax.nn.sigmoid((gl.reshape(B, S, N_BRANCH, D_MODEL) + gate_bias).astype(jnp.float32)).astype(h.dtype)
    merged = gates[:, :, 0] * (oa @ w_branch_na) + gates[:, :, 1] * (ob @ w_branch_swa)
    return merged @ w_out


def peer_ffn(x, w_query, sub_keys, expert_down, expert_up):
    B, S, D = x.shape
    T = B * S
    half = PEER_KEY_DIM // 2

    def chunk(xc):
        q = (xc @ w_query).reshape(PEER_CHUNK, PEER_HEADS, 2, half)
        s = jnp.einsum('thpk,pnk->thpn', q, sub_keys).astype(jnp.float32)
        sv, si = lax.top_k(s, PEER_TOPK)
        cand = (sv[:, :, 0, :, None] + sv[:, :, 1, None, :]).reshape(PEER_CHUNK, PEER_HEADS, PEER_TOPK * PEER_TOPK)
        cidx = (si[:, :, 0, :, None] * PEER_N_KEYS + si[:, :, 1, None, :]).reshape(PEER_CHUNK, PEER_HEADS, PEER_TOPK * PEER_TOPK)
        top_s, sel = lax.top_k(cand, PEER_TOPK)
        eidx = jnp.take_along_axis(cidx, sel, axis=-1)
        g = jax.nn.softmax(top_s, axis=-1)
        u = expert_down[eidx]
        vv = expert_up[eidx]
        act = jax.nn.gelu(jnp.einsum('td,thkd->thk', xc, u).astype(jnp.float32), approximate=False)
        return jnp.einsum('thk,thkd->td', (g * act).astype(x.dtype), vv)

    out = lax.map(chunk, x.reshape(T // PEER_CHUNK, PEER_CHUNK, D))
    return out.reshape(B, S, D)


def setup_inputs(seed: int = 0) -> dict:
    key = jax.random.key(seed)
    ks = jax.random.split(key, 16)
    f32 = jnp.float32

    def nrm(k, shape, scale):
        return jax.random.normal(k, shape, f32) * scale

    return {
        "x": nrm(ks[0], (BATCH, SEQ, D_MODEL), 1.0),
        "norm_mix": 1.0 + nrm(ks[1], (DEPTH, D_MODEL), 0.02),
        "w_in": nrm(ks[2], (DEPTH, D_MODEL, IN_COLS), D_MODEL ** -0.5),
        "gate_bias": nrm(ks[3], (DEPTH, N_BRANCH, D_MODEL), 0.02),
        "qk_norm": 1.0 + nrm(ks[4], (DEPTH, 4, HEAD_DIM), 0.02),
        "na_rpb": nrm(ks[5], (DEPTH, NA_HEADS, 2 * NA_WIN_ROWS - 1, 2 * NA_WIN_COLS - 1), 0.1),
        "swa_sink": nrm(ks[6], (DEPTH, GQA_Q_HEADS), 0.5),
        "w_branch_na": nrm(ks[7], (DEPTH, NA_WIDTH, D_MODEL), NA_WIDTH ** -0.5),
        "w_branch_swa": nrm(ks[8], (DEPTH, GQA_Q_WIDTH, D_MODEL), GQA_Q_WIDTH ** -0.5),
        "w_out": nrm(ks[9], (DEPTH, D_MODEL, D_MODEL), D_MODEL ** -0.5),
        "norm_ffn": 1.0 + nrm(ks[10], (DEPTH, D_MODEL), 0.02),
        "peer_query": nrm(ks[11], (DEPTH, D_MODEL, PEER_HEADS * PEER_KEY_DIM), D_MODEL ** -0.5),
        "peer_sub_keys": nrm(ks[12], (DEPTH, 2, PEER_N_KEYS, PEER_KEY_DIM // 2), (PEER_KEY_DIM // 2) ** -0.5),
        "peer_down": nrm(ks[13], (DEPTH, PEER_N_EXPERTS, D_MODEL), D_MODEL ** -0.5),
        "peer_up": nrm(ks[14], (DEPTH, PEER_N_EXPERTS, D_MODEL), (PEER_HEADS * PEER_TOPK) ** -0.5),
    }


def reference(x, norm_mix, w_in, gate_bias, qk_norm, na_rpb, swa_sink, w_branch_na, w_branch_swa,
              w_out, norm_ffn, peer_query, peer_sub_keys, peer_down, peer_up):
    S = x.shape[1]
    pos = jnp.arange(S, dtype=jnp.int32)
    for l in range(DEPTH):
        x = x + token_mixer(rms_norm(x, norm_mix[l]), w_in[l], gate_bias[l], qk_norm[l], na_rpb[l],
                            swa_sink[l], w_branch_na[l], w_branch_swa[l], w_out[l], pos)
        x = x + peer_ffn(rms_norm(x, norm_ffn[l]), peer_query[l], peer_sub_keys[l], peer_down[l], peer_up[l])
    return x
```

```python
import functools

import jax
import jax.numpy as jnp
from jax import lax
from jax.experimental import pallas as pl
from jax.experimental.pallas import tpu as pltpu

F32 = jnp.float32
BF16 = jnp.bfloat16

D_MODEL = 1024
HEAD_DIM = 64
NA_HEADS = 8
GQA_Q_HEADS = 8
GQA_KV_HEADS = 2
GQA_GROUP = GQA_Q_HEADS // GQA_KV_HEADS
GRID_W = 64
NA_WIN_ROWS = 8
NA_WIN_COLS = 16
SWA_WINDOW = 128
SWA_BLOCK = 128
ROPE_THETA = 10000.0
PEER_HEADS = 8
PEER_N_KEYS = 128
PEER_TOPK = 16
PEER_KEY_DIM = 256
EPS = 1e-6

NA_WIDTH = NA_HEADS * HEAD_DIM
GQA_Q_WIDTH = GQA_Q_HEADS * HEAD_DIM
GQA_KV_WIDTH = GQA_KV_HEADS * HEAD_DIM
COL_QA = 0
COL_KA = NA_WIDTH
COL_VA = 2 * NA_WIDTH
COL_QB = 3 * NA_WIDTH
COL_KB = COL_QB + GQA_Q_WIDTH
COL_VB = COL_KB + GQA_KV_WIDTH
COL_GATE = COL_VB + GQA_KV_WIDTH

LANES = 128
NEG_BIG = -1e30
NO_RANK = 99.0
VMEM_LIMIT = 56 * 1024 * 1024

TM_PROJ = 256
TT_ROUTE = 256
TT_EXPERT = 512
ET_EXPERT = 1024
CH_EXPERT = 32

_NT = (((1,), (1,)), ((), ()))


def _params(*sem):
    return pltpu.CompilerParams(dimension_semantics=sem, vmem_limit_bytes=VMEM_LIMIT)


def _rms(x, gain):
    ms = jnp.mean(x * x, axis=-1, keepdims=True)
    return x * lax.rsqrt(ms + EPS) * gain


def _head_rms(y, gain, gmat):
    y2 = y * y
    hi = y2.astype(BF16)
    lo = (y2 - hi.astype(F32)).astype(BF16)
    ms = jnp.dot(hi, gmat, preferred_element_type=F32) + jnp.dot(lo, gmat, preferred_element_type=F32)
    return y * lax.rsqrt(ms + EPS) * gain


def _rotary(y, cos, sin_signed):
    width = y.shape[-1]
    lane = lax.broadcasted_iota(jnp.int32, (1, width), 1)
    first_half = (lane % HEAD_DIM) < (HEAD_DIM // 2)
    partner = jnp.where(first_half, pltpu.roll(y, width - HEAD_DIM // 2, 1), pltpu.roll(y, HEAD_DIM // 2, 1))
    return y * cos + partner * sin_signed


def _swap_halves(y):
    blocks = [pltpu.roll(y[:, c:c + LANES], HEAD_DIM, 1) for c in range(0, y.shape[-1], LANES)]
    return blocks[0] if len(blocks) == 1 else jnp.concatenate(blocks, axis=-1)


def _inproj_kernel(x_ref, g_ref, w_ref, qkn_ref, gb_ref, cos_ref, sin_ref, gmat_ref,
                   qa_ref, ka_ref, va_ref, qb_ref, qbs_ref, kb_ref, vb_ref, gate_ref):
    h = _rms(x_ref[...], g_ref[...]).astype(BF16)

    def proj(c0, width):
        return jnp.dot(h, w_ref[:, c0:c0 + width], preferred_element_type=F32)

    gmat = gmat_ref[...]
    scale = HEAD_DIM ** -0.5
    qa_ref[...] = (_head_rms(proj(COL_QA, NA_WIDTH), qkn_ref[0:1, :], gmat) * scale).astype(BF16)
    ka_ref[...] = _head_rms(proj(COL_KA, NA_WIDTH), qkn_ref[1:2, :], gmat).astype(BF16)
    va_ref[...] = proj(COL_VA, NA_WIDTH).astype(BF16)
    cos = cos_ref[...]
    sin = sin_ref[...]
    qb = _rotary(_head_rms(proj(COL_QB, GQA_Q_WIDTH), qkn_ref[2:3, :], gmat), cos, sin) * scale
    qb_ref[...] = qb.astype(BF16)
    qbs_ref[...] = _swap_halves(qb).astype(BF16)
    kb = _head_rms(proj(COL_KB, GQA_KV_WIDTH), qkn_ref[3:4, :GQA_KV_WIDTH], gmat[:GQA_KV_WIDTH, :GQA_KV_WIDTH])
    kb_ref[...] = _rotary(kb, cos[:, :GQA_KV_WIDTH], sin[:, :GQA_KV_WIDTH]).astype(BF16)
    vb_ref[...] = proj(COL_VB, GQA_KV_WIDTH).astype(BF16)
    gate_ref[...] = jax.nn.sigmoid(proj(COL_GATE, 2 * D_MODEL) + gb_ref[...])


def _inproj(x, gain, w, qkn, gate_bias, cos, sin, gmat, seq):
    t = x.shape[0]
    tm = TM_PROJ
    pos_blocks = seq // tm
    row = lambda i: (i, 0)
    fixed = lambda i: (0, 0)
    out_w = (NA_WIDTH, NA_WIDTH, NA_WIDTH, GQA_Q_WIDTH, GQA_Q_WIDTH, GQA_KV_WIDTH, GQA_KV_WIDTH)
    out_shape = [jax.ShapeDtypeStruct((t, w_), BF16) for w_ in out_w] + [jax.ShapeDtypeStruct((t, 2 * D_MODEL), F32)]
    out_specs = [pl.BlockSpec((tm, w_), row) for w_ in out_w] + [pl.BlockSpec((tm, 2 * D_MODEL), row)]
    return pl.pallas_call(
        _inproj_kernel,
        grid=(t // tm,),
        in_specs=[
            pl.BlockSpec((tm, D_MODEL), row),
            pl.BlockSpec((1, D_MODEL), fixed),
            pl.BlockSpec(w.shape, fixed),
            pl.BlockSpec(qkn.shape, fixed),
            pl.BlockSpec((1, 2 * D_MODEL), fixed),
            pl.BlockSpec((tm, GQA_Q_WIDTH), lambda i: (i % pos_blocks, 0)),
            pl.BlockSpec((tm, GQA_Q_WIDTH), lambda i: (i % pos_blocks, 0)),
            pl.BlockSpec(gmat.shape, fixed),
        ],
        out_specs=out_specs,
        out_shape=out_shape,
        compiler_params=_params("parallel"),
        name="inproj",
    )(x, gain, w, qkn, gate_bias, cos, sin, gmat)


def _na_kernel(q_ref, k_ref, v_ref, bias_ref, o_ref, *, rows):
    r = pl.program_id(1)
    first_key_row = jnp.clip(r - NA_WIN_ROWS // 2, 0, rows - NA_WIN_ROWS)
    start = pl.multiple_of(first_key_row * GRID_W, GRID_W)
    nkeys = NA_WIN_ROWS * GRID_W
    lane = lax.broadcasted_iota(jnp.int32, (1, LANES), 1)
    low = lane < HEAD_DIM
    for p in range(NA_HEADS // 2):
        cols = slice(p * LANES, (p + 1) * LANES)
        q_pair = q_ref[:, cols]
        k_win = k_ref[pl.ds(start, nkeys), cols]
        v_win = v_ref[pl.ds(start, nkeys), cols]
        outs = []
        for half in range(2):
            q_head = jnp.where(low if half == 0 else jnp.logical_not(low), q_pair, jnp.zeros_like(q_pair))
            s = lax.dot_general(q_head, k_win, _NT, preferred_element_type=F32) + bias_ref[2 * p + half, 0]
            m = jnp.max(s, axis=-1, keepdims=True)
            e = jnp.exp(s - m)
            denom = jnp.sum(e, axis=-1, keepdims=True)
            outs.append(jnp.dot(e.astype(BF16), v_win, preferred_element_type=F32) / denom)
        o_ref[:, cols] = jnp.where(low, outs[0], outs[1]).astype(BF16)


def _bias_class(r, rows):
    half = NA_WIN_ROWS // 2
    return jnp.where(r < half, r, jnp.where(r > rows - half, r - (rows - NA_WIN_ROWS), half))


def _na_attention(q, k, v, bias, batch, seq):
    rows = seq // GRID_W
    return pl.pallas_call(
        functools.partial(_na_kernel, rows=rows),
        grid=(batch, rows),
        in_specs=[
            pl.BlockSpec((GRID_W, NA_WIDTH), lambda b, r: (b * rows + r, 0)),
            pl.BlockSpec((seq, NA_WIDTH), lambda b, r: (b, 0)),
            pl.BlockSpec((seq, NA_WIDTH), lambda b, r: (b, 0)),
            pl.BlockSpec((NA_HEADS, 1, GRID_W, NA_WIN_ROWS * GRID_W), lambda b, r: (0, _bias_class(r, rows), 0, 0)),
        ],
        out_specs=pl.BlockSpec((GRID_W, NA_WIDTH), lambda b, r: (b * rows + r, 0)),
        out_shape=jax.ShapeDtypeStruct(q.shape, BF16),
        compiler_params=_params("parallel", "arbitrary"),
        name="na",
    )(q, k, v, bias)


def _na_bias_tiles(rpb, rows):
    half = NA_WIN_ROWS // 2
    cls_rows = jnp.array(list(range(half)) + [half] + list(range(rows - half + 1, rows)), jnp.int32)
    rs = jnp.clip(cls_rows - half, 0, rows - NA_WIN_ROWS)
    dr_idx = rs[:, None] + jnp.arange(NA_WIN_ROWS)[None, :] - cls_rows[:, None] + (NA_WIN_ROWS - 1)
    c = jnp.arange(GRID_W)
    cs = jnp.clip(c - NA_WIN_COLS // 2, 0, GRID_W - NA_WIN_COLS)
    dc_idx = jnp.clip(c[None, :] - c[:, None] + NA_WIN_COLS - 1, 0, 2 * NA_WIN_COLS - 2)
    in_win = (c[None, :] >= cs[:, None]) & (c[None, :] < cs[:, None] + NA_WIN_COLS)
    bias = rpb.astype(F32)[:, dr_idx[:, None, :, None], dc_idx[None, :, None, :]]
    bias = jnp.where(in_win[None, None, :, None, :], bias, NEG_BIG)
    return bias.reshape(rpb.shape[0], cls_rows.shape[0], GRID_W, NA_WIN_ROWS * GRID_W)


def _swa_kernel(sink_ref, q_ref, qs_ref, k_ref, v_ref, o_ref, *, seq):
    n = pl.program_id(1)
    nkeys = 3 * SWA_BLOCK
    start = pl.multiple_of(jnp.clip((n - 1) * SWA_BLOCK, 0, seq - nkeys), SWA_BLOCK)
    k_win = k_ref[pl.ds(start, nkeys), :]
    v_win = v_ref[pl.ds(start, nkeys), :]
    qpos = n * SWA_BLOCK + lax.broadcasted_iota(jnp.int32, (SWA_BLOCK, nkeys), 0)
    kpos = start + lax.broadcasted_iota(jnp.int32, (SWA_BLOCK, nkeys), 1)
    valid = jnp.abs(qpos - kpos) <= SWA_WINDOW
    lane = lax.broadcasted_iota(jnp.int32, (1, LANES), 1)
    low = lane < HEAD_DIM
    for p in range(GQA_Q_HEADS // 2):
        cols = slice(p * LANES, (p + 1) * LANES)
        outs = []
        for half in range(2):
            head = 2 * p + half
            kv = head // GQA_GROUP
            src = q_ref if half == kv else qs_ref
            q_pair = src[:, cols]
            q_head = jnp.where(low if kv == 0 else jnp.logical_not(low), q_pair, jnp.zeros_like(q_pair))
            s = lax.dot_general(q_head, k_win, _NT, preferred_element_type=F32)
            s = jnp.where(valid, s, NEG_BIG)
            sink = sink_ref[head]
            m = jnp.maximum(jnp.max(s, axis=-1, keepdims=True), sink)
            e = jnp.exp(s - m)
            denom = jnp.sum(e, axis=-1, keepdims=True) + jnp.exp(sink - m)
            o = jnp.dot(e.astype(BF16), v_win, preferred_element_type=F32) / denom
            outs.append(o if half == kv else pltpu.roll(o, HEAD_DIM, 1))
        o_ref[:, cols] = jnp.where(low, outs[0], outs[1]).astype(BF16)


def _swa_attention(sink, q, qs, k, v, batch, seq):
    nb = seq // SWA_BLOCK
    return pl.pallas_call(
        functools.partial(_swa_kernel, seq=seq),
        grid=(batch, nb),
        in_specs=[
            pl.BlockSpec(memory_space=pltpu.SMEM),
            pl.BlockSpec((SWA_BLOCK, GQA_Q_WIDTH), lambda b, n: (b * nb + n, 0)),
            pl.BlockSpec((SWA_BLOCK, GQA_Q_WIDTH), lambda b, n: (b * nb + n, 0)),
            pl.BlockSpec((seq, GQA_KV_WIDTH), lambda b, n: (b, 0)),
            pl.BlockSpec((seq, GQA_KV_WIDTH), lambda b, n: (b, 0)),
        ],
        out_specs=pl.BlockSpec((SWA_BLOCK, GQA_Q_WIDTH), lambda b, n: (b * nb + n, 0)),
        out_shape=jax.ShapeDtypeStruct(q.shape, BF16),
        compiler_params=_params("parallel", "arbitrary"),
        name="swa",
    )(sink, q, qs, k, v)


def _merge_kernel(x_ref, oa_ref, ob_ref, gate_ref, wna_ref, wswa_ref, wout_ref, gffn_ref, wq_ref,
                  xnew_ref, hn_ref, qp_ref):
    ya = jnp.dot(oa_ref[...], wna_ref[...], preferred_element_type=F32)
    yb = jnp.dot(ob_ref[...], wswa_ref[...], preferred_element_type=F32)
    merged = gate_ref[:, :D_MODEL] * ya + gate_ref[:, D_MODEL:] * yb
    xn = x_ref[...] + jnp.dot(merged.astype(BF16), wout_ref[...], preferred_element_type=F32)
    xnew_ref[...] = xn
    hn = _rms(xn, gffn_ref[...]).astype(BF16)
    hn_ref[...] = hn
    qp_ref[...] = jnp.dot(hn, wq_ref[...], preferred_element_type=F32)


def _merge(x, oa, ob, gates, wna, wswa, wout, gffn, wq):
    t = x.shape[0]
    tm = TM_PROJ
    row = lambda i: (i, 0)
    fixed = lambda i: (0, 0)
    nq = wq.shape[1]
    return pl.pallas_call(
        _merge_kernel,
        grid=(t // tm,),
        in_specs=[
            pl.BlockSpec((tm, D_MODEL), row),
            pl.BlockSpec((tm, NA_WIDTH), row),
            pl.BlockSpec((tm, GQA_Q_WIDTH), row),
            pl.BlockSpec((tm, 2 * D_MODEL), row),
            pl.BlockSpec(wna.shape, fixed),
            pl.BlockSpec(wswa.shape, fixed),
            pl.BlockSpec(wout.shape, fixed),
            pl.BlockSpec((1, D_MODEL), fixed),
            pl.BlockSpec(wq.shape, fixed),
        ],
        out_specs=[pl.BlockSpec((tm, D_MODEL), row), pl.BlockSpec((tm, D_MODEL), row), pl.BlockSpec((tm, nq), row)],
        out_shape=[jax.ShapeDtypeStruct((t, D_MODEL), F32), jax.ShapeDtypeStruct((t, D_MODEL), BF16),
                   jax.ShapeDtypeStruct((t, nq), F32)],
        compiler_params=_params("parallel"),
        name="merge",
    )(x, oa, ob, gates, wna, wswa, wout, gffn, wq)


def _top16(s):
    rank = jnp.full(s.shape, NO_RANK, F32)
    rows = []
    for it in range(PEER_TOPK):
        m = jnp.max(s, axis=0, keepdims=True)
        hit = s == m
        rank = jnp.where(hit, float(it), rank)
        s = jnp.where(hit, -jnp.inf, s)
        rows.append(m)
    return rows, rank


def _route_kernel(qp_ref, keys_ref, n_ref, a_ref, r2_ref, b_ref):
    half = PEER_KEY_DIM // 2
    q = qp_ref[...].astype(BF16)
    s1 = lax.dot_general(keys_ref[0].astype(BF16), q[:, :half], _NT, preferred_element_type=F32)
    s2 = lax.dot_general(keys_ref[1].astype(BF16), q[:, half:], _NT, preferred_element_type=F32)
    v1, rank1 = _top16(s1)
    v2, rank2 = _top16(s2)
    tt = s1.shape[1]
    sub = lax.broadcasted_iota(jnp.int32, (PEER_TOPK, tt), 0)
    v2m = jnp.zeros((PEER_TOPK, tt), F32)
    for it in range(PEER_TOPK):
        v2m = jnp.where(sub == it, v2[it], v2m)

    def pair_sums():
        return [v1[a] + v2m for a in range(PEER_TOPK)]

    cand = pair_sums()
    tau = None
    for it in range(PEER_TOPK):
        mx = cand[0]
        for a in range(1, PEER_TOPK):
            mx = jnp.maximum(mx, cand[a])
        tau = jnp.max(mx, axis=0, keepdims=True)
        if it + 1 < PEER_TOPK:
            cand = [jnp.where(c == tau, -jnp.inf, c) for c in cand]

    top = v1[0] + v2[0]
    z = jnp.zeros((1, tt), F32)
    n_i = jnp.zeros(s1.shape, F32)
    for a, c in enumerate(pair_sums()):
        sel = c >= tau
        z = z + jnp.sum(jnp.where(sel, jnp.exp(c - top), 0.0), axis=0, keepdims=True)
        n_a = jnp.sum(jnp.where(sel, 1.0, 0.0), axis=0, keepdims=True)
        n_i = jnp.where(rank1 == float(a), n_a, n_i)
    n_ref[0] = n_i
    a_ref[0] = jnp.exp(s1 - v1[0]) / z
    r2_ref[0] = rank2
    b_ref[0] = jnp.exp(s2 - v2[0])


def _route(qp, sub_keys):
    t = qp.shape[0]
    tt = TT_ROUTE
    out = jax.ShapeDtypeStruct((PEER_HEADS, PEER_N_KEYS, t), F32)
    spec = pl.BlockSpec((1, PEER_N_KEYS, tt), lambda i, h: (h, 0, i))
    return pl.pallas_call(
        _route_kernel,
        grid=(t // tt, PEER_HEADS),
        in_specs=[
            pl.BlockSpec((tt, PEER_KEY_DIM), lambda i, h: (i, h)),
            pl.BlockSpec(sub_keys.shape, lambda i, h: (0, 0, 0)),
        ],
        out_specs=[spec, spec, spec, spec],
        out_shape=[out, out, out, out],
        compiler_params=_params("parallel", "arbitrary"),
        name="route",
    )(qp, sub_keys)


def _gelu(x):
    return 0.5 * x * (1.0 + lax.erf(x * (2.0 ** -0.5)))


def _expert_kernel(hn_ref, u_ref, vt_ref, n_ref, a_ref, r2_ref, b_ref, x_ref, out_ref, acc_ref, act_ref, p_ref):
    e = pl.program_id(1)

    @pl.when(e == 0)
    def _():
        acc_ref[...] = jnp.zeros_like(acc_ref)

    act_ref[...] = lax.dot_general(u_ref[...], hn_ref[...], _NT, preferred_element_type=F32)
    tt = act_ref.shape[1]
    for ii in range(ET_EXPERT // PEER_N_KEYS):

        def chunk(jc, carry, ii=ii):
            j0 = pl.multiple_of(jc * CH_EXPERT, CH_EXPERT)
            w = jnp.zeros((CH_EXPERT, tt), F32)
            for h in range(PEER_HEADS):
                sel = r2_ref[h, pl.ds(j0, CH_EXPERT), :] < n_ref[h, ii:ii + 1, :]
                w = w + jnp.where(sel, b_ref[h, pl.ds(j0, CH_EXPERT), :], 0.0) * a_ref[h, ii:ii + 1, :]
            rows = pl.ds(ii * PEER_N_KEYS + j0, CH_EXPERT)
            p_ref[rows, :] = (w * _gelu(act_ref[rows, :])).astype(BF16)
            return carry

        lax.fori_loop(0, PEER_N_KEYS // CH_EXPERT, chunk, 0)
    acc_ref[...] += jnp.dot(vt_ref[...], p_ref[...], preferred_element_type=F32)

    @pl.when(e == pl.num_programs(1) - 1)
    def _():
        out_ref[...] = x_ref[...] + acc_ref[...].T


def _expert(hn, u, vt, n_i, a_i, r2, b_j, x):
    t = hn.shape[0]
    n_exp = u.shape[0]
    tt = TT_EXPERT
    et = ET_EXPERT
    rows_per_step = et // PEER_N_KEYS
    tok = lambda i, e: (i, 0)
    per_i = pl.BlockSpec((PEER_HEADS, rows_per_step, tt), lambda i, e: (0, e, i))
    per_j = pl.BlockSpec((PEER_HEADS, PEER_N_KEYS, tt), lambda i, e: (0, 0, i))
    return pl.pallas_call(
        _expert_kernel,
        grid=(t // tt, n_exp // et),
        in_specs=[
            pl.BlockSpec((tt, D_MODEL), tok),
            pl.BlockSpec((et, D_MODEL), lambda i, e: (e, 0)),
            pl.BlockSpec((D_MODEL, et), lambda i, e: (0, e)),
            per_i, per_i, per_j, per_j,
            pl.BlockSpec((tt, D_MODEL), tok),
        ],
        out_specs=pl.BlockSpec((tt, D_MODEL), tok),
        out_shape=jax.ShapeDtypeStruct((t, D_MODEL), F32),
        scratch_shapes=[pltpu.VMEM((D_MODEL, tt), F32), pltpu.VMEM((et, tt), F32), pltpu.VMEM((et, tt), BF16)],
        compiler_params=_params("parallel", "arbitrary"),
        name="expert",
    )(hn, u, vt, n_i, a_i, r2, b_j, x)


def _rope_tables(seq):
    half = HEAD_DIM // 2
    inv = ROPE_THETA ** (-jnp.arange(half, dtype=F32) / half)
    ang = jnp.arange(seq, dtype=jnp.int32).astype(F32)[:, None] * inv[None, :]
    cos = jnp.cos(ang)
    sin = jnp.sin(ang)
    cos_h = jnp.concatenate([cos, cos], axis=-1)
    sin_h = jnp.concatenate([-sin, sin], axis=-1)
    return jnp.tile(cos_h, (1, GQA_Q_HEADS)), jnp.tile(sin_h, (1, GQA_Q_HEADS))


def kernel(x, norm_mix, w_in, gate_bias, qk_norm, na_rpb, swa_sink, w_branch_na, w_branch_swa, w_out, norm_ffn,
           peer_query, peer_sub_keys, peer_down, peer_up):
    batch, seq, d = x.shape
    t = batch * seq
    depth = w_in.shape[0]
    rows = seq // GRID_W
    cos, sin = _rope_tables(seq)
    lane_head = jnp.arange(NA_WIDTH) // HEAD_DIM
    gmat = jnp.where(lane_head[:, None] == lane_head[None, :], 1.0 / HEAD_DIM, 0.0).astype(BF16)
    xf = x.reshape(t, d)
    for l in range(depth):
        qkn = jnp.tile(qk_norm[l], (1, NA_HEADS))
        qa, ka, va, qb, qbs, kb, vb, gates = _inproj(
            xf, norm_mix[l][None, :], w_in[l].astype(BF16), qkn, gate_bias[l].reshape(1, -1), cos, sin, gmat, seq)
        oa = _na_attention(qa, ka, va, _na_bias_tiles(na_rpb[l], rows), batch, seq)
        ob = _swa_attention(swa_sink[l], qb, qbs, kb, vb, batch, seq)
        xf, hn, qp = _merge(xf, oa, ob, gates, w_branch_na[l].astype(BF16), w_branch_swa[l].astype(BF16),
                            w_out[l].astype(BF16), norm_ffn[l][None, :], peer_query[l].astype(BF16))
        n_i, a_i, r2, b_j = _route(qp, peer_sub_keys[l])
        xf = _expert(hn, peer_down[l].astype(BF16), peer_up[l].T.astype(BF16), n_i, a_i, r2, b_j, xf)
    return xf.reshape(batch, seq, d)
```

```python
import functools

import jax
import jax.numpy as jnp
import numpy as np
from jax import lax
from jax.experimental import pallas as pl
from jax.experimental.pallas import tpu as pltpu

F32 = jnp.float32
BF16 = jnp.bfloat16

D_MODEL = 1024
HEAD_DIM = 64
NA_HEADS = 8
GQA_Q_HEADS = 8
GQA_KV_HEADS = 2
GQA_GROUP = GQA_Q_HEADS // GQA_KV_HEADS
GRID_W = 64
NA_WIN_ROWS = 8
NA_WIN_COLS = 16
SWA_WINDOW = 128
SWA_BLOCK = 128
ROPE_THETA = 10000.0
PEER_HEADS = 8
PEER_N_KEYS = 128
PEER_TOPK = 16
PEER_KEY_DIM = 256
EPS = 1e-6

NA_WIDTH = NA_HEADS * HEAD_DIM
GQA_Q_WIDTH = GQA_Q_HEADS * HEAD_DIM
GQA_KV_WIDTH = GQA_KV_HEADS * HEAD_DIM
COL_QA = 0
COL_KA = NA_WIDTH
COL_VA = 2 * NA_WIDTH
COL_QB = 3 * NA_WIDTH
COL_KB = COL_QB + GQA_Q_WIDTH
COL_VB = COL_KB + GQA_KV_WIDTH
COL_GATE = COL_VB + GQA_KV_WIDTH

LANES = 128
NEG_BIG = -1e30
NO_RANK = 99.0
VMEM_LIMIT = 56 * 1024 * 1024

TM_PROJ = 256
TT_ROUTE = 256
TT_EXPERT = 512
ET_EXPERT = 1024
LW_EXPERT = 256

_NT = (((1,), (1,)), ((), ()))


def _params(*sem):
    return pltpu.CompilerParams(dimension_semantics=sem, vmem_limit_bytes=VMEM_LIMIT)


def _rms(x, gain):
    ms = jnp.mean(x * x, axis=-1, keepdims=True)
    return x * lax.rsqrt(ms + EPS) * gain


def _head_rms(y, gain, gmat):
    y2 = y * y
    hi = y2.astype(BF16)
    lo = (y2 - hi.astype(F32)).astype(BF16)
    ms = jnp.dot(hi, gmat, preferred_element_type=F32) + jnp.dot(lo, gmat, preferred_element_type=F32)
    return y * lax.rsqrt(ms + EPS) * gain


def _rotary(y, cos, sin_signed):
    width = y.shape[-1]
    lane = lax.broadcasted_iota(jnp.int32, (1, width), 1)
    first_half = (lane % HEAD_DIM) < (HEAD_DIM // 2)
    partner = jnp.where(first_half, pltpu.roll(y, width - HEAD_DIM // 2, 1), pltpu.roll(y, HEAD_DIM // 2, 1))
    return y * cos + partner * sin_signed


def _swap_halves(y):
    blocks = [pltpu.roll(y[:, c:c + LANES], HEAD_DIM, 1) for c in range(0, y.shape[-1], LANES)]
    return blocks[0] if len(blocks) == 1 else jnp.concatenate(blocks, axis=-1)


def _inproj_kernel(x_ref, g_ref, w_ref, qkn_ref, gb_ref, cos_ref, sin_ref, gmat_ref,
                   qa_ref, ka_ref, va_ref, qb_ref, qbs_ref, kb_ref, vb_ref, gate_ref):
    h = _rms(x_ref[...], g_ref[...]).astype(BF16)

    def proj(c0, width):
        return jnp.dot(h, w_ref[:, c0:c0 + width], preferred_element_type=F32)

    gmat = gmat_ref[...]
    scale = HEAD_DIM ** -0.5
    qa_ref[...] = (_head_rms(proj(COL_QA, NA_WIDTH), qkn_ref[0:1, :], gmat) * scale).astype(BF16)
    ka_ref[...] = _head_rms(proj(COL_KA, NA_WIDTH), qkn_ref[1:2, :], gmat).astype(BF16)
    va_ref[...] = proj(COL_VA, NA_WIDTH).astype(BF16)
    cos = cos_ref[...]
    sin = sin_ref[...]
    qb = _rotary(_head_rms(proj(COL_QB, GQA_Q_WIDTH), qkn_ref[2:3, :], gmat), cos, sin) * scale
    qb_ref[...] = qb.astype(BF16)
    qbs_ref[...] = _swap_halves(qb).astype(BF16)
    kb = _head_rms(proj(COL_KB, GQA_KV_WIDTH), qkn_ref[3:4, :GQA_KV_WIDTH], gmat[:GQA_KV_WIDTH, :GQA_KV_WIDTH])
    kb_ref[...] = _rotary(kb, cos[:, :GQA_KV_WIDTH], sin[:, :GQA_KV_WIDTH]).astype(BF16)
    vb_ref[...] = proj(COL_VB, GQA_KV_WIDTH).astype(BF16)
    gate_ref[...] = jax.nn.sigmoid(proj(COL_GATE, 2 * D_MODEL) + gb_ref[...])


def _inproj(x, gain, w, qkn, gate_bias, cos, sin, gmat, seq):
    t = x.shape[0]
    tm = TM_PROJ
    pos_blocks = seq // tm
    row = lambda i: (i, 0)
    fixed = lambda i: (0, 0)
    out_w = (NA_WIDTH, NA_WIDTH, NA_WIDTH, GQA_Q_WIDTH, GQA_Q_WIDTH, GQA_KV_WIDTH, GQA_KV_WIDTH)
    out_shape = [jax.ShapeDtypeStruct((t, w_), BF16) for w_ in out_w] + [jax.ShapeDtypeStruct((t, 2 * D_MODEL), F32)]
    out_specs = [pl.BlockSpec((tm, w_), row) for w_ in out_w] + [pl.BlockSpec((tm, 2 * D_MODEL), row)]
    return pl.pallas_call(
        _inproj_kernel,
        grid=(t // tm,),
        in_specs=[
            pl.BlockSpec((tm, D_MODEL), row),
            pl.BlockSpec((1, D_MODEL), fixed),
            pl.BlockSpec(w.shape, fixed),
            pl.BlockSpec(qkn.shape, fixed),
            pl.BlockSpec((1, 2 * D_MODEL), fixed),
            pl.BlockSpec((tm, GQA_Q_WIDTH), lambda i: (i % pos_blocks, 0)),
            pl.BlockSpec((tm, GQA_Q_WIDTH), lambda i: (i % pos_blocks, 0)),
            pl.BlockSpec(gmat.shape, fixed),
        ],
        out_specs=out_specs,
        out_shape=out_shape,
        compiler_params=_params("parallel"),
        name="inproj",
    )(x, gain, w, qkn, gate_bias, cos, sin, gmat)


def _na_kernel(q_ref, k_ref, v_ref, bias_ref, o_ref, *, rows):
    r = pl.program_id(1)
    first_key_row = jnp.clip(r - NA_WIN_ROWS // 2, 0, rows - NA_WIN_ROWS)
    start = pl.multiple_of(first_key_row * GRID_W, GRID_W)
    nkeys = NA_WIN_ROWS * GRID_W
    lane = lax.broadcasted_iota(jnp.int32, (1, LANES), 1)
    low = lane < HEAD_DIM
    for p in range(NA_HEADS // 2):
        cols = slice(p * LANES, (p + 1) * LANES)
        q_pair = q_ref[:, cols]
        k_win = k_ref[pl.ds(start, nkeys), cols]
        v_win = v_ref[pl.ds(start, nkeys), cols]
        outs = []
        for half in range(2):
            q_head = jnp.where(low if half == 0 else jnp.logical_not(low), q_pair, jnp.zeros_like(q_pair))
            s = lax.dot_general(q_head, k_win, _NT, preferred_element_type=F32) + bias_ref[2 * p + half, 0]
            m = jnp.max(s, axis=-1, keepdims=True)
            e = jnp.exp(s - m)
            denom = jnp.sum(e, axis=-1, keepdims=True)
            outs.append(jnp.dot(e.astype(BF16), v_win, preferred_element_type=F32) / denom)
        o_ref[:, cols] = jnp.where(low, outs[0], outs[1]).astype(BF16)


def _bias_class(r, rows):
    half = NA_WIN_ROWS // 2
    return jnp.where(r < half, r, jnp.where(r > rows - half, r - (rows - NA_WIN_ROWS), half))


def _na_attention(q, k, v, bias, batch, seq):
    rows = seq // GRID_W
    return pl.pallas_call(
        functools.partial(_na_kernel, rows=rows),
        grid=(batch, rows),
        in_specs=[
            pl.BlockSpec((GRID_W, NA_WIDTH), lambda b, r: (b * rows + r, 0)),
            pl.BlockSpec((seq, NA_WIDTH), lambda b, r: (b, 0)),
            pl.BlockSpec((seq, NA_WIDTH), lambda b, r: (b, 0)),
            pl.BlockSpec((NA_HEADS, 1, GRID_W, NA_WIN_ROWS * GRID_W), lambda b, r: (0, _bias_class(r, rows), 0, 0)),
        ],
        out_specs=pl.BlockSpec((GRID_W, NA_WIDTH), lambda b, r: (b * rows + r, 0)),
        out_shape=jax.ShapeDtypeStruct(q.shape, BF16),
        compiler_params=_params("parallel", "arbitrary"),
        name="na",
    )(q, k, v, bias)


def _na_bias_tiles(rpb, rows):
    half = NA_WIN_ROWS // 2
    cls_rows = np.array(list(range(half)) + [half] + list(range(rows - half + 1, rows)))
    rs = np.clip(cls_rows - half, 0, rows - NA_WIN_ROWS)
    dr_idx = rs[:, None] + np.arange(NA_WIN_ROWS)[None, :] - cls_rows[:, None] + (NA_WIN_ROWS - 1)
    c = np.arange(GRID_W)
    cs = np.clip(c - NA_WIN_COLS // 2, 0, GRID_W - NA_WIN_COLS)
    dc_idx = np.clip(c[None, :] - c[:, None] + NA_WIN_COLS - 1, 0, 2 * NA_WIN_COLS - 2)
    in_win = (c[None, :] >= cs[:, None]) & (c[None, :] < cs[:, None] + NA_WIN_COLS)
    onehot = (dc_idx[None] == np.arange(2 * NA_WIN_COLS - 1)[:, None, None]).astype(np.float32)
    toep = jnp.einsum("hrc,cqk->hrqk", rpb.astype(F32), onehot, precision=lax.Precision.HIGHEST)
    toep = jnp.where(in_win[None, None], toep, NEG_BIG)
    tiles = jnp.stack([jnp.stack([toep[:, int(dr_idx[k, a])] for a in range(NA_WIN_ROWS)], axis=2)
                       for k in range(len(cls_rows))], axis=1)
    return tiles.reshape(rpb.shape[0], len(cls_rows), GRID_W, NA_WIN_ROWS * GRID_W)


def _swa_kernel(sink_ref, q_ref, qs_ref, k_ref, v_ref, o_ref, *, seq):
    n = pl.program_id(1)
    nkeys = 3 * SWA_BLOCK
    start = pl.multiple_of(jnp.clip((n - 1) * SWA_BLOCK, 0, seq - nkeys), SWA_BLOCK)
    k_win = k_ref[pl.ds(start, nkeys), :]
    v_win = v_ref[pl.ds(start, nkeys), :]
    qpos = n * SWA_BLOCK + lax.broadcasted_iota(jnp.int32, (SWA_BLOCK, nkeys), 0)
    kpos = start + lax.broadcasted_iota(jnp.int32, (SWA_BLOCK, nkeys), 1)
    valid = jnp.abs(qpos - kpos) <= SWA_WINDOW
    lane = lax.broadcasted_iota(jnp.int32, (1, LANES), 1)
    low = lane < HEAD_DIM
    for p in range(GQA_Q_HEADS // 2):
        cols = slice(p * LANES, (p + 1) * LANES)
        outs = []
        for half in range(2):
            head = 2 * p + half
            kv = head // GQA_GROUP
            src = q_ref if half == kv else qs_ref
            q_pair = src[:, cols]
            q_head = jnp.where(low if kv == 0 else jnp.logical_not(low), q_pair, jnp.zeros_like(q_pair))
            s = lax.dot_general(q_head, k_win, _NT, preferred_element_type=F32)
            s = jnp.where(valid, s, NEG_BIG)
            sink = sink_ref[head]
            m = jnp.maximum(jnp.max(s, axis=-1, keepdims=True), sink)
            e = jnp.exp(s - m)
            denom = jnp.sum(e, axis=-1, keepdims=True) + jnp.exp(sink - m)
            o = jnp.dot(e.astype(BF16), v_win, preferred_element_type=F32) / denom
            outs.append(o if half == kv else pltpu.roll(o, HEAD_DIM, 1))
        o_ref[:, cols] = jnp.where(low, outs[0], outs[1]).astype(BF16)


def _swa_attention(sink, q, qs, k, v, batch, seq):
    nb = seq // SWA_BLOCK
    return pl.pallas_call(
        functools.partial(_swa_kernel, seq=seq),
        grid=(batch, nb),
        in_specs=[
            pl.BlockSpec(memory_space=pltpu.SMEM),
            pl.BlockSpec((SWA_BLOCK, GQA_Q_WIDTH), lambda b, n: (b * nb + n, 0)),
            pl.BlockSpec((SWA_BLOCK, GQA_Q_WIDTH), lambda b, n: (b * nb + n, 0)),
            pl.BlockSpec((seq, GQA_KV_WIDTH), lambda b, n: (b, 0)),
            pl.BlockSpec((seq, GQA_KV_WIDTH), lambda b, n: (b, 0)),
        ],
        out_specs=pl.BlockSpec((SWA_BLOCK, GQA_Q_WIDTH), lambda b, n: (b * nb + n, 0)),
        out_shape=jax.ShapeDtypeStruct(q.shape, BF16),
        compiler_params=_params("parallel", "arbitrary"),
        name="swa",
    )(sink, q, qs, k, v)


def _merge_kernel(x_ref, oa_ref, ob_ref, gate_ref, wna_ref, wswa_ref, wout_ref, gffn_ref, wq_ref,
                  xnew_ref, hn_ref, qp_ref):
    ya = jnp.dot(oa_ref[...], wna_ref[...], preferred_element_type=F32)
    yb = jnp.dot(ob_ref[...], wswa_ref[...], preferred_element_type=F32)
    merged = gate_ref[:, :D_MODEL] * ya + gate_ref[:, D_MODEL:] * yb
    xn = x_ref[...] + jnp.dot(merged.astype(BF16), wout_ref[...], preferred_element_type=F32)
    xnew_ref[...] = xn
    hn = _rms(xn, gffn_ref[...]).astype(BF16)
    hn_ref[...] = hn
    qp_ref[...] = jnp.dot(hn, wq_ref[...], preferred_element_type=F32)


def _merge(x, oa, ob, gates, wna, wswa, wout, gffn, wq):
    t = x.shape[0]
    tm = TM_PROJ
    row = lambda i: (i, 0)
    fixed = lambda i: (0, 0)
    nq = wq.shape[1]
    return pl.pallas_call(
        _merge_kernel,
        grid=(t // tm,),
        in_specs=[
            pl.BlockSpec((tm, D_MODEL), row),
            pl.BlockSpec((tm, NA_WIDTH), row),
            pl.BlockSpec((tm, GQA_Q_WIDTH), row),
            pl.BlockSpec((tm, 2 * D_MODEL), row),
            pl.BlockSpec(wna.shape, fixed),
            pl.BlockSpec(wswa.shape, fixed),
            pl.BlockSpec(wout.shape, fixed),
            pl.BlockSpec((1, D_MODEL), fixed),
            pl.BlockSpec(wq.shape, fixed),
        ],
        out_specs=[pl.BlockSpec((tm, D_MODEL), row), pl.BlockSpec((tm, D_MODEL), row), pl.BlockSpec((tm, nq), row)],
        out_shape=[jax.ShapeDtypeStruct((t, D_MODEL), F32), jax.ShapeDtypeStruct((t, D_MODEL), BF16),
                   jax.ShapeDtypeStruct((t, nq), F32)],
        compiler_params=_params("parallel"),
        name="merge",
    )(x, oa, ob, gates, wna, wswa, wout, gffn, wq)


def _top16(s):
    rank = jnp.full(s.shape, NO_RANK, F32)
    rows = []
    for it in range(PEER_TOPK):
        m = jnp.max(s, axis=0, keepdims=True)
        hit = s == m
        rank = jnp.where(hit, float(it), rank)
        s = jnp.where(hit, -jnp.inf, s)
        rows.append(m)
    return rows, rank


def _route_kernel(qp_ref, keys_ref, n_ref, a_ref, r2_ref, b_ref):
    half = PEER_KEY_DIM // 2
    q = qp_ref[...].astype(BF16)
    s1 = lax.dot_general(keys_ref[0].astype(BF16), q[:, :half], _NT, preferred_element_type=F32)
    s2 = lax.dot_general(keys_ref[1].astype(BF16), q[:, half:], _NT, preferred_element_type=F32)
    v1, rank1 = _top16(s1)
    v2, rank2 = _top16(s2)
    tt = s1.shape[1]
    sub = lax.broadcasted_iota(jnp.int32, (PEER_TOPK, tt), 0)
    v2m = jnp.zeros((PEER_TOPK, tt), F32)
    for it in range(PEER_TOPK):
        v2m = jnp.where(sub == it, v2[it], v2m)

    def pair_sums():
        return [v1[a] + v2m for a in range(PEER_TOPK)]

    cand = pair_sums()
    tau = None
    for it in range(PEER_TOPK):
        mx = cand[0]
        for a in range(1, PEER_TOPK):
            mx = jnp.maximum(mx, cand[a])
        tau = jnp.max(mx, axis=0, keepdims=True)
        if it + 1 < PEER_TOPK:
            cand = [jnp.where(c == tau, -jnp.inf, c) for c in cand]

    top = v1[0] + v2[0]
    z = jnp.zeros((1, tt), F32)
    n_i = jnp.zeros(s1.shape, F32)
    for a, c in enumerate(pair_sums()):
        sel = c >= tau
        z = z + jnp.sum(jnp.where(sel, jnp.exp(c - top), 0.0), axis=0, keepdims=True)
        n_a = jnp.sum(jnp.where(sel, 1.0, 0.0), axis=0, keepdims=True)
        n_i = jnp.where(rank1 == float(a), n_a, n_i)
    n_ref[0] = n_i
    a_ref[0] = jnp.exp(s1 - v1[0]) / z
    r2_ref[0] = rank2.astype(BF16)
    b_ref[0] = jnp.exp(s2 - v2[0]).astype(BF16)


def _route(qp, sub_keys):
    t = qp.shape[0]
    tt = TT_ROUTE
    out = jax.ShapeDtypeStruct((PEER_HEADS, PEER_N_KEYS, t), F32)
    out16 = jax.ShapeDtypeStruct((PEER_HEADS, PEER_N_KEYS, t), BF16)
    spec = pl.BlockSpec((1, PEER_N_KEYS, tt), lambda i, h: (h, 0, i))
    return pl.pallas_call(
        _route_kernel,
        grid=(t // tt, PEER_HEADS),
        in_specs=[
            pl.BlockSpec((tt, PEER_KEY_DIM), lambda i, h: (i, h)),
            pl.BlockSpec(sub_keys.shape, lambda i, h: (0, 0, 0)),
        ],
        out_specs=[spec, spec, spec, spec],
        out_shape=[out, out, out16, out16],
        compiler_params=_params("parallel", "arbitrary"),
        name="route",
    )(qp, sub_keys)


def _gelu(x):
    return 0.5 * x * (1.0 + lax.erf(x * (2.0 ** -0.5)))


def _expert_kernel(hn_ref, u_ref, vt_ref, n_ref, a_ref, r2_ref, b_ref, x_ref, out_ref, acc_ref, act_ref, p_ref):
    e = pl.program_id(1)

    @pl.when(e == 0)
    def _():
        acc_ref[...] = jnp.zeros_like(acc_ref)

    act_ref[...] = lax.dot_general(u_ref[...], hn_ref[...], _NT, preferred_element_type=F32)
    tt = act_ref.shape[1]
    zero = jnp.zeros((PEER_N_KEYS, LW_EXPERT), BF16)
    for ii in range(ET_EXPERT // PEER_N_KEYS):
        rows = slice(ii * PEER_N_KEYS, (ii + 1) * PEER_N_KEYS)
        for l0 in range(0, tt, LW_EXPERT):
            lanes = slice(l0, l0 + LW_EXPERT)
            w = zero
            for h in range(PEER_HEADS):
                sel = r2_ref[h, :, lanes] < n_ref[h, ii:ii + 1, lanes].astype(BF16)
                w = w + jnp.where(sel, b_ref[h, :, lanes], zero) * a_ref[h, ii:ii + 1, lanes].astype(BF16)
            p_ref[rows, lanes] = w * _gelu(act_ref[rows, lanes]).astype(BF16)
    acc_ref[...] += jnp.dot(vt_ref[...], p_ref[...], preferred_element_type=F32)

    @pl.when(e == pl.num_programs(1) - 1)
    def _():
        out_ref[...] = x_ref[...] + acc_ref[...].T


def _expert(hn, u, vt, n_i, a_i, r2, b_j, x):
    t = hn.shape[0]
    n_exp = u.shape[0]
    tt = TT_EXPERT
    et = ET_EXPERT
    rows_per_step = et // PEER_N_KEYS
    tok = lambda i, e: (i, 0)
    per_i = pl.BlockSpec((PEER_HEADS, rows_per_step, tt), lambda i, e: (0, e, i))
    per_j = pl.BlockSpec((PEER_HEADS, PEER_N_KEYS, tt), lambda i, e: (0, 0, i))
    return pl.pallas_call(
        _expert_kernel,
        grid=(t // tt, n_exp // et),
        in_specs=[
            pl.BlockSpec((tt, D_MODEL), tok),
            pl.BlockSpec((et, D_MODEL), lambda i, e: (e, 0)),
            pl.BlockSpec((D_MODEL, et), lambda i, e: (0, e)),
            per_i, per_i, per_j, per_j,
            pl.BlockSpec((tt, D_MODEL), tok),
        ],
        out_specs=pl.BlockSpec((tt, D_MODEL), tok),
        out_shape=jax.ShapeDtypeStruct((t, D_MODEL), F32),
        scratch_shapes=[pltpu.VMEM((D_MODEL, tt), F32), pltpu.VMEM((et, tt), F32), pltpu.VMEM((et, tt), BF16)],
        compiler_params=_params("parallel", "arbitrary"),
        name="expert",
    )(hn, u, vt, n_i, a_i, r2, b_j, x)


def _rope_tables(seq):
    half = HEAD_DIM // 2
    inv = ROPE_THETA ** (-jnp.arange(half, dtype=F32) / half)
    ang = jnp.arange(seq, dtype=jnp.int32).astype(F32)[:, None] * inv[None, :]
    cos = jnp.cos(ang)
    sin = jnp.sin(ang)
    cos_h = jnp.concatenate([cos, cos], axis=-1)
    sin_h = jnp.concatenate([-sin, sin], axis=-1)
    return jnp.tile(cos_h, (1, GQA_Q_HEADS)), jnp.tile(sin_h, (1, GQA_Q_HEADS))


def kernel(x, norm_mix, w_in, gate_bias, qk_norm, na_rpb, swa_sink, w_branch_na, w_branch_swa, w_out, norm_ffn,
           peer_query, peer_sub_keys, peer_down, peer_up):
    batch, seq, d = x.shape
    t = batch * seq
    depth = w_in.shape[0]
    rows = seq // GRID_W
    cos, sin = _rope_tables(seq)
    lane_head = jnp.arange(NA_WIDTH) // HEAD_DIM
    gmat = jnp.where(lane_head[:, None] == lane_head[None, :], 1.0 / HEAD_DIM, 0.0).astype(BF16)
    xf = x.reshape(t, d)
    for l in range(depth):
        qkn = jnp.tile(qk_norm[l], (1, NA_HEADS))
        qa, ka, va, qb, qbs, kb, vb, gates = _inproj(
            xf, norm_mix[l][None, :], w_in[l].astype(BF16), qkn, gate_bias[l].reshape(1, -1), cos, sin, gmat, seq)
        oa = _na_attention(qa, ka, va, _na_bias_tiles(na_rpb[l], rows), batch, seq)
        ob = _swa_attention(swa_sink[l], qb, qbs, kb, vb, batch, seq)
        xf, hn, qp = _merge(xf, oa, ob, gates, w_branch_na[l].astype(BF16), w_branch_swa[l].astype(BF16),
                            w_out[l].astype(BF16), norm_ffn[l][None, :], peer_query[l].astype(BF16))
        n_i, a_i, r2, b_j = _route(qp, peer_sub_keys[l])
        xf = _expert(hn, peer_down[l].astype(BF16), peer_up[l].T.astype(BF16), n_i, a_i, r2, b_j, xf)
    return xf.reshape(batch, seq, d)
```

```python
import functools

import jax
import jax.numpy as jnp
import numpy as np
from jax import lax
from jax.experimental import pallas as pl
from jax.experimental.pallas import tpu as pltpu

F32 = jnp.float32
BF16 = jnp.bfloat16

D_MODEL = 1024
HEAD_DIM = 64
NA_HEADS = 8
GQA_Q_HEADS = 8
GQA_KV_HEADS = 2
GQA_GROUP = GQA_Q_HEADS // GQA_KV_HEADS
GRID_W = 64
NA_WIN_ROWS = 8
NA_WIN_COLS = 16
SWA_WINDOW = 128
SWA_BLOCK = 128
ROPE_THETA = 10000.0
PEER_HEADS = 8
PEER_N_KEYS = 128
PEER_TOPK = 16
PEER_KEY_DIM = 256
EPS = 1e-6

NA_WIDTH = NA_HEADS * HEAD_DIM
GQA_Q_WIDTH = GQA_Q_HEADS * HEAD_DIM
GQA_KV_WIDTH = GQA_KV_HEADS * HEAD_DIM
COL_QA = 0
COL_KA = NA_WIDTH
COL_VA = 2 * NA_WIDTH
COL_QB = 3 * NA_WIDTH
COL_KB = COL_QB + GQA_Q_WIDTH
COL_VB = COL_KB + GQA_KV_WIDTH
COL_GATE = COL_VB + GQA_KV_WIDTH

LANES = 128
NEG_BIG = -1e30
NO_RANK = 99.0
VMEM_LIMIT = 56 * 1024 * 1024

TM_PROJ = 256
TT_ROUTE = 256
TT_EXPERT = 512
ET_EXPERT = 1024
LW_EXPERT = 256
BF16_ROWS = 16

_NT = (((1,), (1,)), ((), ()))


def _params(*sem):
    return pltpu.CompilerParams(dimension_semantics=sem, vmem_limit_bytes=VMEM_LIMIT)


def _rms(x, gain):
    ms = jnp.mean(x * x, axis=-1, keepdims=True)
    return x * lax.rsqrt(ms + EPS) * gain


def _head_rms(y, gain, gmat):
    y2 = y * y
    hi = y2.astype(BF16)
    lo = (y2 - hi.astype(F32)).astype(BF16)
    ms = jnp.dot(hi, gmat, preferred_element_type=F32) + jnp.dot(lo, gmat, preferred_element_type=F32)
    return y * lax.rsqrt(ms + EPS) * gain


def _rotary(y, cos, sin_signed):
    width = y.shape[-1]
    lane = lax.broadcasted_iota(jnp.int32, (1, width), 1)
    first_half = (lane % HEAD_DIM) < (HEAD_DIM // 2)
    partner = jnp.where(first_half, pltpu.roll(y, width - HEAD_DIM // 2, 1), pltpu.roll(y, HEAD_DIM // 2, 1))
    return y * cos + partner * sin_signed


def _swap_halves(y):
    blocks = [pltpu.roll(y[:, c:c + LANES], HEAD_DIM, 1) for c in range(0, y.shape[-1], LANES)]
    return blocks[0] if len(blocks) == 1 else jnp.concatenate(blocks, axis=-1)


def _inproj_kernel(x_ref, g_ref, w_ref, qkn_ref, gb_ref, cos_ref, sin_ref, gmat_ref,
                   qa_ref, ka_ref, va_ref, qb_ref, qbs_ref, kb_ref, vb_ref, gate_ref):
    h = _rms(x_ref[...], g_ref[...]).astype(BF16)

    def proj(c0, width):
        return jnp.dot(h, w_ref[:, c0:c0 + width], preferred_element_type=F32)

    gmat = gmat_ref[...]
    scale = HEAD_DIM ** -0.5
    qa_ref[...] = (_head_rms(proj(COL_QA, NA_WIDTH), qkn_ref[0:1, :], gmat) * scale).astype(BF16)
    ka_ref[...] = _head_rms(proj(COL_KA, NA_WIDTH), qkn_ref[1:2, :], gmat).astype(BF16)
    va_ref[...] = proj(COL_VA, NA_WIDTH).astype(BF16)
    cos = cos_ref[...]
    sin = sin_ref[...]
    qb = _rotary(_head_rms(proj(COL_QB, GQA_Q_WIDTH), qkn_ref[2:3, :], gmat), cos, sin) * scale
    qb_ref[...] = qb.astype(BF16)
    qbs_ref[...] = _swap_halves(qb).astype(BF16)
    kb = _head_rms(proj(COL_KB, GQA_KV_WIDTH), qkn_ref[3:4, :GQA_KV_WIDTH], gmat[:GQA_KV_WIDTH, :GQA_KV_WIDTH])
    kb_ref[...] = _rotary(kb, cos[:, :GQA_KV_WIDTH], sin[:, :GQA_KV_WIDTH]).astype(BF16)
    vb_ref[...] = proj(COL_VB, GQA_KV_WIDTH).astype(BF16)
    gate_ref[...] = jax.nn.sigmoid(proj(COL_GATE, 2 * D_MODEL) + gb_ref[...])


def _inproj(x, gain, w, qkn, gate_bias, cos, sin, gmat, seq):
    t = x.shape[0]
    tm = TM_PROJ
    pos_blocks = seq // tm
    row = lambda i: (i, 0)
    fixed = lambda i: (0, 0)
    out_w = (NA_WIDTH, NA_WIDTH, NA_WIDTH, GQA_Q_WIDTH, GQA_Q_WIDTH, GQA_KV_WIDTH, GQA_KV_WIDTH)
    out_shape = [jax.ShapeDtypeStruct((t, w_), BF16) for w_ in out_w] + [jax.ShapeDtypeStruct((t, 2 * D_MODEL), F32)]
    out_specs = [pl.BlockSpec((tm, w_), row) for w_ in out_w] + [pl.BlockSpec((tm, 2 * D_MODEL), row)]
    return pl.pallas_call(
        _inproj_kernel,
        grid=(t // tm,),
        in_specs=[
            pl.BlockSpec((tm, D_MODEL), row),
            pl.BlockSpec((1, D_MODEL), fixed),
            pl.BlockSpec(w.shape, fixed),
            pl.BlockSpec(qkn.shape, fixed),
            pl.BlockSpec((1, 2 * D_MODEL), fixed),
            pl.BlockSpec((tm, GQA_Q_WIDTH), lambda i: (i % pos_blocks, 0)),
            pl.BlockSpec((tm, GQA_Q_WIDTH), lambda i: (i % pos_blocks, 0)),
            pl.BlockSpec(gmat.shape, fixed),
        ],
        out_specs=out_specs,
        out_shape=out_shape,
        compiler_params=_params("parallel"),
        name="inproj",
    )(x, gain, w, qkn, gate_bias, cos, sin, gmat)


def _na_kernel(q_ref, k_ref, v_ref, bias_ref, o_ref, *, rows):
    r = pl.program_id(1)
    first_key_row = jnp.clip(r - NA_WIN_ROWS // 2, 0, rows - NA_WIN_ROWS)
    start = pl.multiple_of(first_key_row * GRID_W, GRID_W)
    nkeys = NA_WIN_ROWS * GRID_W
    lane = lax.broadcasted_iota(jnp.int32, (1, LANES), 1)
    low = lane < HEAD_DIM
    scores = []
    for head in range(NA_HEADS):
        cols = slice((head // 2) * LANES, (head // 2 + 1) * LANES)
        q_pair = q_ref[:, cols]
        q_head = jnp.where(low if head % 2 == 0 else jnp.logical_not(low), q_pair, jnp.zeros_like(q_pair))
        k_win = k_ref[pl.ds(start, nkeys), cols]
        scores.append(lax.dot_general(q_head, k_win, _NT, preferred_element_type=F32) + bias_ref[head, 0])
    probs = []
    for s in scores:
        e = jnp.exp(s - jnp.max(s, axis=-1, keepdims=True))
        probs.append((e.astype(BF16), jnp.sum(e, axis=-1, keepdims=True)))
    outs = []
    for head, (e, denom) in enumerate(probs):
        cols = slice((head // 2) * LANES, (head // 2 + 1) * LANES)
        v_win = v_ref[pl.ds(start, nkeys), cols]
        outs.append(jnp.dot(e, v_win, preferred_element_type=F32) / denom)
    for p in range(NA_HEADS // 2):
        o_ref[:, p * LANES:(p + 1) * LANES] = jnp.where(low, outs[2 * p], outs[2 * p + 1]).astype(BF16)


def _bias_class(r, rows):
    half = NA_WIN_ROWS // 2
    return jnp.where(r < half, r, jnp.where(r > rows - half, r - (rows - NA_WIN_ROWS), half))


def _na_attention(q, k, v, bias, batch, seq):
    rows = seq // GRID_W
    return pl.pallas_call(
        functools.partial(_na_kernel, rows=rows),
        grid=(batch, rows),
        in_specs=[
            pl.BlockSpec((GRID_W, NA_WIDTH), lambda b, r: (b * rows + r, 0)),
            pl.BlockSpec((seq, NA_WIDTH), lambda b, r: (b, 0)),
            pl.BlockSpec((seq, NA_WIDTH), lambda b, r: (b, 0)),
            pl.BlockSpec((NA_HEADS, 1, GRID_W, NA_WIN_ROWS * GRID_W), lambda b, r: (0, _bias_class(r, rows), 0, 0)),
        ],
        out_specs=pl.BlockSpec((GRID_W, NA_WIDTH), lambda b, r: (b * rows + r, 0)),
        out_shape=jax.ShapeDtypeStruct(q.shape, BF16),
        compiler_params=_params("parallel", "arbitrary"),
        name="na",
    )(q, k, v, bias)


def _na_bias_tiles(rpb, rows):
    half = NA_WIN_ROWS // 2
    cls_rows = np.array(list(range(half)) + [half] + list(range(rows - half + 1, rows)))
    rs = np.clip(cls_rows - half, 0, rows - NA_WIN_ROWS)
    dr_idx = rs[:, None] + np.arange(NA_WIN_ROWS)[None, :] - cls_rows[:, None] + (NA_WIN_ROWS - 1)
    c = np.arange(GRID_W)
    cs = np.clip(c - NA_WIN_COLS // 2, 0, GRID_W - NA_WIN_COLS)
    dc_idx = np.clip(c[None, :] - c[:, None] + NA_WIN_COLS - 1, 0, 2 * NA_WIN_COLS - 2)
    in_win = (c[None, :] >= cs[:, None]) & (c[None, :] < cs[:, None] + NA_WIN_COLS)
    onehot = (dc_idx[None] == np.arange(2 * NA_WIN_COLS - 1)[:, None, None]).astype(np.float32)
    toep = jnp.einsum("hrc,cqk->hrqk", rpb.astype(F32), onehot, precision=lax.Precision.HIGHEST)
    toep = jnp.where(in_win[None, None], toep, NEG_BIG)
    tiles = jnp.stack([jnp.stack([toep[:, int(dr_idx[k, a])] for a in range(NA_WIN_ROWS)], axis=2)
                       for k in range(len(cls_rows))], axis=1)
    return tiles.reshape(rpb.shape[0], len(cls_rows), GRID_W, NA_WIN_ROWS * GRID_W)


def _swa_kernel(sink_ref, q_ref, qs_ref, k_ref, v_ref, o_ref, *, seq):
    n = pl.program_id(1)
    nkeys = 3 * SWA_BLOCK
    start = pl.multiple_of(jnp.clip((n - 1) * SWA_BLOCK, 0, seq - nkeys), SWA_BLOCK)
    k_win = k_ref[pl.ds(start, nkeys), :]
    v_win = v_ref[pl.ds(start, nkeys), :]
    qpos = n * SWA_BLOCK + lax.broadcasted_iota(jnp.int32, (SWA_BLOCK, nkeys), 0)
    kpos = start + lax.broadcasted_iota(jnp.int32, (SWA_BLOCK, nkeys), 1)
    valid = jnp.abs(qpos - kpos) <= SWA_WINDOW
    lane = lax.broadcasted_iota(jnp.int32, (1, LANES), 1)
    low = lane < HEAD_DIM
    scores = []
    for head in range(GQA_Q_HEADS):
        cols = slice((head // 2) * LANES, (head // 2 + 1) * LANES)
        kv = head // GQA_GROUP
        q_pair = (q_ref if head % 2 == kv else qs_ref)[:, cols]
        q_head = jnp.where(low if kv == 0 else jnp.logical_not(low), q_pair, jnp.zeros_like(q_pair))
        s = lax.dot_general(q_head, k_win, _NT, preferred_element_type=F32)
        scores.append(jnp.where(valid, s, NEG_BIG))
    probs = []
    for head, s in enumerate(scores):
        sink = sink_ref[head]
        m = jnp.maximum(jnp.max(s, axis=-1, keepdims=True), sink)
        e = jnp.exp(s - m)
        probs.append((e.astype(BF16), jnp.sum(e, axis=-1, keepdims=True) + jnp.exp(sink - m)))
    outs = []
    for head, (e, denom) in enumerate(probs):
        o = jnp.dot(e, v_win, preferred_element_type=F32) / denom
        outs.append(o if head % 2 == head // GQA_GROUP else pltpu.roll(o, HEAD_DIM, 1))
    for p in range(GQA_Q_HEADS // 2):
        o_ref[:, p * LANES:(p + 1) * LANES] = jnp.where(low, outs[2 * p], outs[2 * p + 1]).astype(BF16)


def _swa_attention(sink, q, qs, k, v, batch, seq):
    nb = seq // SWA_BLOCK
    return pl.pallas_call(
        functools.partial(_swa_kernel, seq=seq),
        grid=(batch, nb),
        in_specs=[
            pl.BlockSpec(memory_space=pltpu.SMEM),
            pl.BlockSpec((SWA_BLOCK, GQA_Q_WIDTH), lambda b, n: (b * nb + n, 0)),
            pl.BlockSpec((SWA_BLOCK, GQA_Q_WIDTH), lambda b, n: (b * nb + n, 0)),
            pl.BlockSpec((seq, GQA_KV_WIDTH), lambda b, n: (b, 0)),
            pl.BlockSpec((seq, GQA_KV_WIDTH), lambda b, n: (b, 0)),
        ],
        out_specs=pl.BlockSpec((SWA_BLOCK, GQA_Q_WIDTH), lambda b, n: (b * nb + n, 0)),
        out_shape=jax.ShapeDtypeStruct(q.shape, BF16),
        compiler_params=_params("parallel", "arbitrary"),
        name="swa",
    )(sink, q, qs, k, v)


def _merge_kernel(x_ref, oa_ref, ob_ref, gate_ref, wna_ref, wswa_ref, wout_ref, gffn_ref, wq_ref,
                  xnew_ref, hn_ref, qp_ref):
    ya = jnp.dot(oa_ref[...], wna_ref[...], preferred_element_type=F32)
    yb = jnp.dot(ob_ref[...], wswa_ref[...], preferred_element_type=F32)
    merged = gate_ref[:, :D_MODEL] * ya + gate_ref[:, D_MODEL:] * yb
    xn = x_ref[...] + jnp.dot(merged.astype(BF16), wout_ref[...], preferred_element_type=F32)
    xnew_ref[...] = xn
    hn = _rms(xn, gffn_ref[...]).astype(BF16)
    hn_ref[...] = hn
    qp_ref[...] = jnp.dot(hn, wq_ref[...], preferred_element_type=F32)


def _merge(x, oa, ob, gates, wna, wswa, wout, gffn, wq):
    t = x.shape[0]
    tm = TM_PROJ
    row = lambda i: (i, 0)
    fixed = lambda i: (0, 0)
    nq = wq.shape[1]
    return pl.pallas_call(
        _merge_kernel,
        grid=(t // tm,),
        in_specs=[
            pl.BlockSpec((tm, D_MODEL), row),
            pl.BlockSpec((tm, NA_WIDTH), row),
            pl.BlockSpec((tm, GQA_Q_WIDTH), row),
            pl.BlockSpec((tm, 2 * D_MODEL), row),
            pl.BlockSpec(wna.shape, fixed),
            pl.BlockSpec(wswa.shape, fixed),
            pl.BlockSpec(wout.shape, fixed),
            pl.BlockSpec((1, D_MODEL), fixed),
            pl.BlockSpec(wq.shape, fixed),
        ],
        out_specs=[pl.BlockSpec((tm, D_MODEL), row), pl.BlockSpec((tm, D_MODEL), row), pl.BlockSpec((tm, nq), row)],
        out_shape=[jax.ShapeDtypeStruct((t, D_MODEL), F32), jax.ShapeDtypeStruct((t, D_MODEL), BF16),
                   jax.ShapeDtypeStruct((t, nq), F32)],
        compiler_params=_params("parallel"),
        name="merge",
    )(x, oa, ob, gates, wna, wswa, wout, gffn, wq)


def _top16(s):
    rank = jnp.full(s.shape, NO_RANK, F32)
    rows = []
    for it in range(PEER_TOPK):
        m = jnp.max(s, axis=0, keepdims=True)
        hit = s == m
        rank = jnp.where(hit, float(it), rank)
        s = jnp.where(hit, -jnp.inf, s)
        rows.append(m)
    return rows, rank


def _route_kernel(qp_ref, keys_ref, n_ref, a_ref, r2_ref, b_ref):
    half = PEER_KEY_DIM // 2
    q = qp_ref[...].astype(BF16)
    s1 = lax.dot_general(keys_ref[0].astype(BF16), q[:, :half], _NT, preferred_element_type=F32)
    s2 = lax.dot_general(keys_ref[1].astype(BF16), q[:, half:], _NT, preferred_element_type=F32)
    v1, rank1 = _top16(s1)
    v2, rank2 = _top16(s2)
    tt = s1.shape[1]
    sub = lax.broadcasted_iota(jnp.int32, (PEER_TOPK, tt), 0)
    v2m = jnp.zeros((PEER_TOPK, tt), F32)
    for it in range(PEER_TOPK):
        v2m = jnp.where(sub == it, v2[it], v2m)

    def pair_sums():
        return [v1[a] + v2m for a in range(PEER_TOPK)]

    cand = pair_sums()
    tau = None
    for it in range(PEER_TOPK):
        mx = cand[0]
        for a in range(1, PEER_TOPK):
            mx = jnp.maximum(mx, cand[a])
        tau = jnp.max(mx, axis=0, keepdims=True)
        if it + 1 < PEER_TOPK:
            cand = [jnp.where(c == tau, -jnp.inf, c) for c in cand]

    top = v1[0] + v2[0]
    z = jnp.zeros((1, tt), F32)
    n_i = jnp.zeros(s1.shape, F32)
    for a, c in enumerate(pair_sums()):
        sel = c >= tau
        z = z + jnp.sum(jnp.where(sel, jnp.exp(c - top), 0.0), axis=0, keepdims=True)
        n_a = jnp.sum(jnp.where(sel, 1.0, 0.0), axis=0, keepdims=True)
        n_i = jnp.where(rank1 == float(a), n_a, n_i)
    n_ref[0] = n_i
    a_ref[0] = jnp.exp(s1 - v1[0]) / z
    r2_ref[0] = rank2.astype(BF16)
    b_ref[0] = jnp.exp(s2 - v2[0]).astype(BF16)


def _route(qp, sub_keys):
    t = qp.shape[0]
    tt = TT_ROUTE
    out = jax.ShapeDtypeStruct((PEER_HEADS, PEER_N_KEYS, t), F32)
    out16 = jax.ShapeDtypeStruct((PEER_HEADS, PEER_N_KEYS, t), BF16)
    spec = pl.BlockSpec((1, PEER_N_KEYS, tt), lambda i, h: (h, 0, i))
    return pl.pallas_call(
        _route_kernel,
        grid=(t // tt, PEER_HEADS),
        in_specs=[
            pl.BlockSpec((tt, PEER_KEY_DIM), lambda i, h: (i, h)),
            pl.BlockSpec(sub_keys.shape, lambda i, h: (0, 0, 0)),
        ],
        out_specs=[spec, spec, spec, spec],
        out_shape=[out, out, out16, out16],
        compiler_params=_params("parallel", "arbitrary"),
        name="route",
    )(qp, sub_keys)


def _gelu(x):
    return 0.5 * x * (1.0 + lax.erf(x * (2.0 ** -0.5)))


def _gate_slab(k, act_ref, p_ref, n_ref, a_ref, r2_ref, b_ref):
    slabs_per_row = act_ref.shape[1] // LW_EXPERT
    ii = k // slabs_per_row
    l0 = (k % slabs_per_row) * LW_EXPERT
    rows = slice(ii * PEER_N_KEYS, (ii + 1) * PEER_N_KEYS)
    lanes = slice(l0, l0 + LW_EXPERT)
    zero = jnp.zeros((BF16_ROWS, LW_EXPERT), BF16)
    groups = PEER_N_KEYS // BF16_ROWS
    w = [zero] * groups
    for h in range(PEER_HEADS):
        n_b = jnp.broadcast_to(n_ref[h, ii:ii + 1, lanes], (BF16_ROWS, LW_EXPERT)).astype(BF16)
        a_b = jnp.broadcast_to(a_ref[h, ii:ii + 1, lanes], (BF16_ROWS, LW_EXPERT)).astype(BF16)
        for r in range(groups):
            jr = slice(r * BF16_ROWS, (r + 1) * BF16_ROWS)
            w[r] = w[r] + jnp.where(r2_ref[h, jr, lanes] < n_b, b_ref[h, jr, lanes], zero) * a_b
    for r in range(groups):
        er = slice(rows.start + r * BF16_ROWS, rows.start + (r + 1) * BF16_ROWS)
        p_ref[er, lanes] = w[r] * _gelu(act_ref[er, lanes]).astype(BF16)


def _expert_kernel(hn_ref, u_ref, vt_ref, n_ref, a_ref, r2_ref, b_ref, x_ref, out_ref, acc_ref, act_ref, p_ref):
    e = pl.program_id(1)

    @pl.when(e == 0)
    def _():
        acc_ref[...] = jnp.zeros_like(acc_ref)

    act_ref[...] = lax.dot_general(u_ref[...], hn_ref[...], _NT, preferred_element_type=F32)
    for k in range((ET_EXPERT // PEER_N_KEYS) * (act_ref.shape[1] // LW_EXPERT)):
        _gate_slab(k, act_ref, p_ref, n_ref, a_ref, r2_ref, b_ref)
    acc_ref[...] += jnp.dot(vt_ref[...], p_ref[...], preferred_element_type=F32)

    @pl.when(e == pl.num_programs(1) - 1)
    def _():
        out_ref[...] = x_ref[...] + acc_ref[...].T


def _expert(hn, u, vt, n_i, a_i, r2, b_j, x):
    t = hn.shape[0]
    n_exp = u.shape[0]
    tt = TT_EXPERT
    et = ET_EXPERT
    rows_per_step = et // PEER_N_KEYS
    tok = lambda i, e: (i, 0)
    per_i = pl.BlockSpec((PEER_HEADS, rows_per_step, tt), lambda i, e: (0, e, i))
    per_j = pl.BlockSpec((PEER_HEADS, PEER_N_KEYS, tt), lambda i, e: (0, 0, i))
    return pl.pallas_call(
        _expert_kernel,
        grid=(t // tt, n_exp // et),
        in_specs=[
            pl.BlockSpec((tt, D_MODEL), tok),
            pl.BlockSpec((et, D_MODEL), lambda i, e: (e, 0)),
            pl.BlockSpec((D_MODEL, et), lambda i, e: (0, e)),
            per_i, per_i, per_j, per_j,
            pl.BlockSpec((tt, D_MODEL), tok),
        ],
        out_specs=pl.BlockSpec((tt, D_MODEL), tok),
        out_shape=jax.ShapeDtypeStruct((t, D_MODEL), F32),
        scratch_shapes=[pltpu.VMEM((D_MODEL, tt), F32), pltpu.VMEM((et, tt), F32), pltpu.VMEM((et, tt), BF16)],
        compiler_params=_params("parallel", "arbitrary"),
        name="expert",
    )(hn, u, vt, n_i, a_i, r2, b_j, x)


def _rope_tables(seq):
    half = HEAD_DIM // 2
    inv = ROPE_THETA ** (-jnp.arange(half, dtype=F32) / half)
    ang = jnp.arange(seq, dtype=jnp.int32).astype(F32)[:, None] * inv[None, :]
    cos = jnp.cos(ang)
    sin = jnp.sin(ang)
    cos_h = jnp.concatenate([cos, cos], axis=-1)
    sin_h = jnp.concatenate([-sin, sin], axis=-1)
    return jnp.tile(cos_h, (1, GQA_Q_HEADS)), jnp.tile(sin_h, (1, GQA_Q_HEADS))


def kernel(x, norm_mix, w_in, gate_bias, qk_norm, na_rpb, swa_sink, w_branch_na, w_branch_swa, w_out, norm_ffn,
           peer_query, peer_sub_keys, peer_down, peer_up):
    batch, seq, d = x.shape
    t = batch * seq
    depth = w_in.shape[0]
    rows = seq // GRID_W
    cos, sin = _rope_tables(seq)
    lane_head = jnp.arange(NA_WIDTH) // HEAD_DIM
    gmat = jnp.where(lane_head[:, None] == lane_head[None, :], 1.0 / HEAD_DIM, 0.0).astype(BF16)
    xf = x.reshape(t, d)
    for l in range(depth):
        qkn = jnp.tile(qk_norm[l], (1, NA_HEADS))
        qa, ka, va, qb, qbs, kb, vb, gates = _inproj(
            xf, norm_mix[l][None, :], w_in[l].astype(BF16), qkn, gate_bias[l].reshape(1, -1), cos, sin, gmat, seq)
        oa = _na_attention(qa, ka, va, _na_bias_tiles(na_rpb[l], rows), batch, seq)
        ob = _swa_attention(swa_sink[l], qb, qbs, kb, vb, batch, seq)
        xf, hn, qp = _merge(xf, oa, ob, gates, w_branch_na[l].astype(BF16), w_branch_swa[l].astype(BF16),
                            w_out[l].astype(BF16), norm_ffn[l][None, :], peer_query[l].astype(BF16))
        n_i, a_i, r2, b_j = _route(qp, peer_sub_keys[l])
        xf = _expert(hn, peer_down[l].astype(BF16), peer_up[l].T.astype(BF16), n_i, a_i, r2, b_j, xf)
    return xf.reshape(batch, seq, d)
```

```python
import functools

import jax
import jax.numpy as jnp
import numpy as np
from jax import lax
from jax.experimental import pallas as pl
from jax.experimental.pallas import tpu as pltpu

F32 = jnp.float32
BF16 = jnp.bfloat16

D_MODEL = 1024
HEAD_DIM = 64
NA_HEADS = 8
GQA_Q_HEADS = 8
GQA_KV_HEADS = 2
GQA_GROUP = GQA_Q_HEADS // GQA_KV_HEADS
GRID_W = 64
NA_WIN_ROWS = 8
NA_WIN_COLS = 16
SWA_WINDOW = 128
SWA_BLOCK = 128
ROPE_THETA = 10000.0
PEER_HEADS = 8
PEER_N_KEYS = 128
PEER_TOPK = 16
PEER_KEY_DIM = 256
EPS = 1e-6

NA_WIDTH = NA_HEADS * HEAD_DIM
GQA_Q_WIDTH = GQA_Q_HEADS * HEAD_DIM
GQA_KV_WIDTH = GQA_KV_HEADS * HEAD_DIM
COL_QA = 0
COL_KA = NA_WIDTH
COL_VA = 2 * NA_WIDTH
COL_QB = 3 * NA_WIDTH
COL_KB = COL_QB + GQA_Q_WIDTH
COL_VB = COL_KB + GQA_KV_WIDTH
COL_GATE = COL_VB + GQA_KV_WIDTH

LANES = 128
NEG_BIG = -1e30
NO_RANK = 99.0
VMEM_LIMIT = 56 * 1024 * 1024

TM_PROJ = 256
TT_ROUTE = 128
HEADS_ROUTE = 4
TT_EXPERT = 512
ET_EXPERT = 1024
LW_EXPERT = 256
BF16_ROWS = 16
F32_ROWS = 8

_NT = (((1,), (1,)), ((), ()))


def _params(*sem):
    return pltpu.CompilerParams(dimension_semantics=sem, vmem_limit_bytes=VMEM_LIMIT)


def _rms(x, gain):
    ms = jnp.mean(x * x, axis=-1, keepdims=True)
    return x * lax.rsqrt(ms + EPS) * gain


def _head_rms(y, gain, gmat):
    y2 = y * y
    hi = y2.astype(BF16)
    lo = (y2 - hi.astype(F32)).astype(BF16)
    ms = jnp.dot(hi, gmat, preferred_element_type=F32) + jnp.dot(lo, gmat, preferred_element_type=F32)
    return y * lax.rsqrt(ms + EPS) * gain


def _rotary(y, cos, sin_signed):
    width = y.shape[-1]
    lane = lax.broadcasted_iota(jnp.int32, (1, width), 1)
    first_half = (lane % HEAD_DIM) < (HEAD_DIM // 2)
    partner = jnp.where(first_half, pltpu.roll(y, width - HEAD_DIM // 2, 1), pltpu.roll(y, HEAD_DIM // 2, 1))
    return y * cos + partner * sin_signed


def _swap_halves(y):
    blocks = [pltpu.roll(y[:, c:c + LANES], HEAD_DIM, 1) for c in range(0, y.shape[-1], LANES)]
    return blocks[0] if len(blocks) == 1 else jnp.concatenate(blocks, axis=-1)


def _inproj_kernel(x_ref, g_ref, w_ref, qkn_ref, gb_ref, cos_ref, sin_ref, gmat_ref,
                   qa_ref, ka_ref, va_ref, qb_ref, qbs_ref, kb_ref, vb_ref, gate_ref):
    h = _rms(x_ref[...], g_ref[...]).astype(BF16)

    def proj(c0, width):
        return jnp.dot(h, w_ref[:, c0:c0 + width], preferred_element_type=F32)

    gmat = gmat_ref[...]
    scale = HEAD_DIM ** -0.5
    qa_ref[...] = (_head_rms(proj(COL_QA, NA_WIDTH), qkn_ref[0:1, :], gmat) * scale).astype(BF16)
    ka_ref[...] = _head_rms(proj(COL_KA, NA_WIDTH), qkn_ref[1:2, :], gmat).astype(BF16)
    va_ref[...] = proj(COL_VA, NA_WIDTH).astype(BF16)
    cos = cos_ref[...]
    sin = sin_ref[...]
    qb = _rotary(_head_rms(proj(COL_QB, GQA_Q_WIDTH), qkn_ref[2:3, :], gmat), cos, sin) * scale
    qb_ref[...] = qb.astype(BF16)
    qbs_ref[...] = _swap_halves(qb).astype(BF16)
    kb = _head_rms(proj(COL_KB, GQA_KV_WIDTH), qkn_ref[3:4, :GQA_KV_WIDTH], gmat[:GQA_KV_WIDTH, :GQA_KV_WIDTH])
    kb_ref[...] = _rotary(kb, cos[:, :GQA_KV_WIDTH], sin[:, :GQA_KV_WIDTH]).astype(BF16)
    vb_ref[...] = proj(COL_VB, GQA_KV_WIDTH).astype(BF16)
    gate_ref[...] = jax.nn.sigmoid(proj(COL_GATE, 2 * D_MODEL) + gb_ref[...])


def _inproj(x, gain, w, qkn, gate_bias, cos, sin, gmat, seq):
    t = x.shape[0]
    tm = TM_PROJ
    pos_blocks = seq // tm
    row = lambda i: (i, 0)
    fixed = lambda i: (0, 0)
    out_w = (NA_WIDTH, NA_WIDTH, NA_WIDTH, GQA_Q_WIDTH, GQA_Q_WIDTH, GQA_KV_WIDTH, GQA_KV_WIDTH)
    out_shape = [jax.ShapeDtypeStruct((t, w_), BF16) for w_ in out_w] + [jax.ShapeDtypeStruct((t, 2 * D_MODEL), F32)]
    out_specs = [pl.BlockSpec((tm, w_), row) for w_ in out_w] + [pl.BlockSpec((tm, 2 * D_MODEL), row)]
    return pl.pallas_call(
        _inproj_kernel,
        grid=(t // tm,),
        in_specs=[
            pl.BlockSpec((tm, D_MODEL), row),
            pl.BlockSpec((1, D_MODEL), fixed),
            pl.BlockSpec(w.shape, fixed),
            pl.BlockSpec(qkn.shape, fixed),
            pl.BlockSpec((1, 2 * D_MODEL), fixed),
            pl.BlockSpec((tm, GQA_Q_WIDTH), lambda i: (i % pos_blocks, 0)),
            pl.BlockSpec((tm, GQA_Q_WIDTH), lambda i: (i % pos_blocks, 0)),
            pl.BlockSpec(gmat.shape, fixed),
        ],
        out_specs=out_specs,
        out_shape=out_shape,
        compiler_params=_params("parallel"),
        name="inproj",
    )(x, gain, w, qkn, gate_bias, cos, sin, gmat)


def _na_kernel(q_ref, k_ref, v_ref, bias_ref, o_ref, *, rows):
    r = pl.program_id(1)
    first_key_row = jnp.clip(r - NA_WIN_ROWS // 2, 0, rows - NA_WIN_ROWS)
    start = pl.multiple_of(first_key_row * GRID_W, GRID_W)
    nkeys = NA_WIN_ROWS * GRID_W
    lane = lax.broadcasted_iota(jnp.int32, (1, LANES), 1)
    low = lane < HEAD_DIM
    scores = []
    for head in range(NA_HEADS):
        cols = slice((head // 2) * LANES, (head // 2 + 1) * LANES)
        q_pair = q_ref[:, cols]
        q_head = jnp.where(low if head % 2 == 0 else jnp.logical_not(low), q_pair, jnp.zeros_like(q_pair))
        k_win = k_ref[pl.ds(start, nkeys), cols]
        scores.append(lax.dot_general(q_head, k_win, _NT, preferred_element_type=F32) + bias_ref[head, 0])
    probs = []
    for s in scores:
        e = jnp.exp(s - jnp.max(s, axis=-1, keepdims=True))
        probs.append((e.astype(BF16), jnp.sum(e, axis=-1, keepdims=True)))
    outs = []
    for head, (e, denom) in enumerate(probs):
        cols = slice((head // 2) * LANES, (head // 2 + 1) * LANES)
        v_win = v_ref[pl.ds(start, nkeys), cols]
        outs.append(jnp.dot(e, v_win, preferred_element_type=F32) / denom)
    for p in range(NA_HEADS // 2):
        o_ref[:, p * LANES:(p + 1) * LANES] = jnp.where(low, outs[2 * p], outs[2 * p + 1]).astype(BF16)


def _bias_class(r, rows):
    half = NA_WIN_ROWS // 2
    return jnp.where(r < half, r, jnp.where(r > rows - half, r - (rows - NA_WIN_ROWS), half))


def _na_attention(q, k, v, bias, batch, seq):
    rows = seq // GRID_W
    return pl.pallas_call(
        functools.partial(_na_kernel, rows=rows),
        grid=(batch, rows),
        in_specs=[
            pl.BlockSpec((GRID_W, NA_WIDTH), lambda b, r: (b * rows + r, 0)),
            pl.BlockSpec((seq, NA_WIDTH), lambda b, r: (b, 0)),
            pl.BlockSpec((seq, NA_WIDTH), lambda b, r: (b, 0)),
            pl.BlockSpec((NA_HEADS, 1, GRID_W, NA_WIN_ROWS * GRID_W), lambda b, r: (0, _bias_class(r, rows), 0, 0)),
        ],
        out_specs=pl.BlockSpec((GRID_W, NA_WIDTH), lambda b, r: (b * rows + r, 0)),
        out_shape=jax.ShapeDtypeStruct(q.shape, BF16),
        compiler_params=_params("parallel", "arbitrary"),
        name="na",
    )(q, k, v, bias)


def _na_bias_tiles(rpb, rows):
    half = NA_WIN_ROWS // 2
    cls_rows = np.array(list(range(half)) + [half] + list(range(rows - half + 1, rows)))
    rs = np.clip(cls_rows - half, 0, rows - NA_WIN_ROWS)
    dr_idx = rs[:, None] + np.arange(NA_WIN_ROWS)[None, :] - cls_rows[:, None] + (NA_WIN_ROWS - 1)
    c = np.arange(GRID_W)
    cs = np.clip(c - NA_WIN_COLS // 2, 0, GRID_W - NA_WIN_COLS)
    dc_idx = np.clip(c[None, :] - c[:, None] + NA_WIN_COLS - 1, 0, 2 * NA_WIN_COLS - 2)
    in_win = (c[None, :] >= cs[:, None]) & (c[None, :] < cs[:, None] + NA_WIN_COLS)
    onehot = (dc_idx[None] == np.arange(2 * NA_WIN_COLS - 1)[:, None, None]).astype(np.float32)
    toep = jnp.einsum("hrc,cqk->hrqk", rpb.astype(F32), onehot, precision=lax.Precision.HIGHEST)
    toep = jnp.where(in_win[None, None], toep, NEG_BIG)
    tiles = jnp.stack([jnp.stack([toep[:, int(dr_idx[k, a])] for a in range(NA_WIN_ROWS)], axis=2)
                       for k in range(len(cls_rows))], axis=1)
    return tiles.reshape(rpb.shape[0], len(cls_rows), GRID_W, NA_WIN_ROWS * GRID_W)


def _swa_kernel(sink_ref, q_ref, qs_ref, k_ref, v_ref, o_ref, *, seq):
    n = pl.program_id(1)
    nkeys = 3 * SWA_BLOCK
    start = pl.multiple_of(jnp.clip((n - 1) * SWA_BLOCK, 0, seq - nkeys), SWA_BLOCK)
    k_win = k_ref[pl.ds(start, nkeys), :]
    v_win = v_ref[pl.ds(start, nkeys), :]
    qpos = n * SWA_BLOCK + lax.broadcasted_iota(jnp.int32, (SWA_BLOCK, nkeys), 0)
    kpos = start + lax.broadcasted_iota(jnp.int32, (SWA_BLOCK, nkeys), 1)
    valid = jnp.abs(qpos - kpos) <= SWA_WINDOW
    lane = lax.broadcasted_iota(jnp.int32, (1, LANES), 1)
    low = lane < HEAD_DIM
    scores = []
    for head in range(GQA_Q_HEADS):
        cols = slice((head // 2) * LANES, (head // 2 + 1) * LANES)
        kv = head // GQA_GROUP
        q_pair = (q_ref if head % 2 == kv else qs_ref)[:, cols]
        q_head = jnp.where(low if kv == 0 else jnp.logical_not(low), q_pair, jnp.zeros_like(q_pair))
        s = lax.dot_general(q_head, k_win, _NT, preferred_element_type=F32)
        scores.append(jnp.where(valid, s, NEG_BIG))
    probs = []
    for head, s in enumerate(scores):
        sink = sink_ref[head]
        m = jnp.maximum(jnp.max(s, axis=-1, keepdims=True), sink)
        e = jnp.exp(s - m)
        probs.append((e.astype(BF16), jnp.sum(e, axis=-1, keepdims=True) + jnp.exp(sink - m)))
    outs = []
    for head, (e, denom) in enumerate(probs):
        o = jnp.dot(e, v_win, preferred_element_type=F32) / denom
        outs.append(o if head % 2 == head // GQA_GROUP else pltpu.roll(o, HEAD_DIM, 1))
    for p in range(GQA_Q_HEADS // 2):
        o_ref[:, p * LANES:(p + 1) * LANES] = jnp.where(low, outs[2 * p], outs[2 * p + 1]).astype(BF16)


def _swa_attention(sink, q, qs, k, v, batch, seq):
    nb = seq // SWA_BLOCK
    return pl.pallas_call(
        functools.partial(_swa_kernel, seq=seq),
        grid=(batch, nb),
        in_specs=[
            pl.BlockSpec(memory_space=pltpu.SMEM),
            pl.BlockSpec((SWA_BLOCK, GQA_Q_WIDTH), lambda b, n: (b * nb + n, 0)),
            pl.BlockSpec((SWA_BLOCK, GQA_Q_WIDTH), lambda b, n: (b * nb + n, 0)),
            pl.BlockSpec((seq, GQA_KV_WIDTH), lambda b, n: (b, 0)),
            pl.BlockSpec((seq, GQA_KV_WIDTH), lambda b, n: (b, 0)),
        ],
        out_specs=pl.BlockSpec((SWA_BLOCK, GQA_Q_WIDTH), lambda b, n: (b * nb + n, 0)),
        out_shape=jax.ShapeDtypeStruct(q.shape, BF16),
        compiler_params=_params("parallel", "arbitrary"),
        name="swa",
    )(sink, q, qs, k, v)


def _merge_kernel(x_ref, oa_ref, ob_ref, gate_ref, wna_ref, wswa_ref, wout_ref, gffn_ref, wq_ref,
                  xnew_ref, hn_ref, qp_ref):
    ya = jnp.dot(oa_ref[...], wna_ref[...], preferred_element_type=F32)
    yb = jnp.dot(ob_ref[...], wswa_ref[...], preferred_element_type=F32)
    merged = gate_ref[:, :D_MODEL] * ya + gate_ref[:, D_MODEL:] * yb
    xn = x_ref[...] + jnp.dot(merged.astype(BF16), wout_ref[...], preferred_element_type=F32)
    xnew_ref[...] = xn
    hn = _rms(xn, gffn_ref[...]).astype(BF16)
    hn_ref[...] = hn
    qp_ref[...] = jnp.dot(hn, wq_ref[...], preferred_element_type=F32)


def _merge(x, oa, ob, gates, wna, wswa, wout, gffn, wq):
    t = x.shape[0]
    tm = TM_PROJ
    row = lambda i: (i, 0)
    fixed = lambda i: (0, 0)
    nq = wq.shape[1]
    return pl.pallas_call(
        _merge_kernel,
        grid=(t // tm,),
        in_specs=[
            pl.BlockSpec((tm, D_MODEL), row),
            pl.BlockSpec((tm, NA_WIDTH), row),
            pl.BlockSpec((tm, GQA_Q_WIDTH), row),
            pl.BlockSpec((tm, 2 * D_MODEL), row),
            pl.BlockSpec(wna.shape, fixed),
            pl.BlockSpec(wswa.shape, fixed),
            pl.BlockSpec(wout.shape, fixed),
            pl.BlockSpec((1, D_MODEL), fixed),
            pl.BlockSpec(wq.shape, fixed),
        ],
        out_specs=[pl.BlockSpec((tm, D_MODEL), row), pl.BlockSpec((tm, D_MODEL), row), pl.BlockSpec((tm, nq), row)],
        out_shape=[jax.ShapeDtypeStruct((t, D_MODEL), F32), jax.ShapeDtypeStruct((t, D_MODEL), BF16),
                   jax.ShapeDtypeStruct((t, nq), F32)],
        compiler_params=_params("parallel"),
        name="merge",
    )(x, oa, ob, gates, wna, wswa, wout, gffn, wq)


def _top16_many(arrays):
    arrays = list(arrays)
    ranks = [jnp.full(s.shape, NO_RANK, F32) for s in arrays]
    rows = [[] for _ in arrays]
    for it in range(PEER_TOPK):
        for k, s in enumerate(arrays):
            m = jnp.max(s, axis=0, keepdims=True)
            hit = s == m
            ranks[k] = jnp.where(hit, float(it), ranks[k])
            arrays[k] = jnp.where(hit, -jnp.inf, s)
            rows[k].append(m)
    return rows, ranks


def _route_kernel(qp_ref, keys_ref, n_ref, a_ref, r2_ref, b_ref):
    half = PEER_KEY_DIM // 2
    heads = n_ref.shape[0]
    keys = [keys_ref[0].astype(BF16), keys_ref[1].astype(BF16)]
    scores = []
    for h in range(heads):
        for side in range(2):
            q = qp_ref[:, h * PEER_KEY_DIM + side * half:h * PEER_KEY_DIM + (side + 1) * half].astype(BF16)
            scores.append(lax.dot_general(keys[side], q, _NT, preferred_element_type=F32))
    top_rows, ranks = _top16_many(scores)
    tt = scores[0].shape[1]
    sub = lax.broadcasted_iota(jnp.int32, (F32_ROWS, tt), 0)
    first4 = sub < 4

    def rows8(vals):
        out = jnp.zeros((F32_ROWS, tt), F32)
        for r, v in enumerate(vals):
            out = jnp.where(sub == r, v, out)
        return out

    groups = []
    for h in range(heads):
        v1, v2 = top_rows[2 * h], top_rows[2 * h + 1]
        v2lo = rows8(v2[:8])
        v2dup = rows8(v2[:4] + v2[:4])
        groups.append([
            (v1[0] + v2lo, (0, 0)), (v1[0] + rows8(v2[8:]), (0, 0)), (v1[1] + v2lo, (1, 1)), (v1[2] + v2lo, (2, 2)),
            (jnp.where(first4, v1[3], v1[4]) + v2dup, (3, 4)), (jnp.where(first4, v1[5], v1[6]) + v2dup, (5, 6)),
            (v1[7] + v2lo, (7, 7)), (rows8(v1[8:]) + v2[0], None),
        ])
    cands = [[g for g, _ in head_groups] for head_groups in groups]
    taus = [None] * heads
    for it in range(PEER_TOPK):
        for h in range(heads):
            mx = cands[h][0]
            for c in cands[h][1:]:
                mx = jnp.maximum(mx, c)
            taus[h] = jnp.max(mx, axis=0, keepdims=True)
            if it + 1 < PEER_TOPK:
                cands[h] = [jnp.where(c == taus[h], -jnp.inf, c) for c in cands[h]]

    for h in range(heads):
        v1, v2 = top_rows[2 * h], top_rows[2 * h + 1]
        s1, s2 = scores[2 * h], scores[2 * h + 1]
        top = v1[0] + v2[0]
        z = jnp.zeros((1, tt), F32)
        n_a = [jnp.zeros((1, tt), F32) for _ in range(PEER_TOPK)]
        for g, owners in groups[h]:
            sel = g >= taus[h]
            z = z + jnp.sum(jnp.where(sel, jnp.exp(g - top), 0.0), axis=0, keepdims=True)
            ones = jnp.where(sel, 1.0, 0.0)
            if owners is None:
                for r in range(F32_ROWS):
                    n_a[F32_ROWS + r] = ones[r:r + 1, :]
            elif owners[0] == owners[1]:
                n_a[owners[0]] = n_a[owners[0]] + jnp.sum(ones, axis=0, keepdims=True)
            else:
                n_a[owners[0]] = jnp.sum(jnp.where(first4, ones, 0.0), axis=0, keepdims=True)
                n_a[owners[1]] = jnp.sum(jnp.where(first4, 0.0, ones), axis=0, keepdims=True)
        n_i = jnp.zeros(s1.shape, F32)
        for a in range(PEER_TOPK):
            n_i = jnp.where(ranks[2 * h] == float(a), n_a[a], n_i)
        n_ref[h] = n_i
        a_ref[h] = jnp.exp(s1 - v1[0]) / z
        r2_ref[h] = ranks[2 * h + 1].astype(BF16)
        b_ref[h] = jnp.exp(s2 - v2[0]).astype(BF16)


def _route(qp, sub_keys):
    t = qp.shape[0]
    tt = TT_ROUTE
    out = jax.ShapeDtypeStruct((PEER_HEADS, PEER_N_KEYS, t), F32)
    out16 = jax.ShapeDtypeStruct((PEER_HEADS, PEER_N_KEYS, t), BF16)
    hs = HEADS_ROUTE
    spec = pl.BlockSpec((hs, PEER_N_KEYS, tt), lambda i, h: (h, 0, i))
    return pl.pallas_call(
        _route_kernel,
        grid=(t // tt, PEER_HEADS // hs),
        in_specs=[
            pl.BlockSpec((tt, hs * PEER_KEY_DIM), lambda i, h: (i, h)),
            pl.BlockSpec(sub_keys.shape, lambda i, h: (0, 0, 0)),
        ],
        out_specs=[spec, spec, spec, spec],
        out_shape=[out, out, out16, out16],
        compiler_params=_params("parallel", "arbitrary"),
        name="route",
    )(qp, sub_keys)


def _gelu(x):
    return 0.5 * x * (1.0 + lax.erf(x * (2.0 ** -0.5)))


def _gate_slab(k, act_ref, p_ref, n_ref, a_ref, r2_ref, b_ref):
    slabs_per_row = act_ref.shape[1] // LW_EXPERT
    ii = k // slabs_per_row
    l0 = (k % slabs_per_row) * LW_EXPERT
    rows = slice(ii * PEER_N_KEYS, (ii + 1) * PEER_N_KEYS)
    lanes = slice(l0, l0 + LW_EXPERT)
    zero = jnp.zeros((BF16_ROWS, LW_EXPERT), BF16)
    groups = PEER_N_KEYS // BF16_ROWS
    w = [zero] * groups
    for h in range(PEER_HEADS):
        n_b = jnp.broadcast_to(n_ref[h, ii:ii + 1, lanes], (BF16_ROWS, LW_EXPERT)).astype(BF16)
        a_b = jnp.broadcast_to(a_ref[h, ii:ii + 1, lanes], (BF16_ROWS, LW_EXPERT)).astype(BF16)
        for r in range(groups):
            jr = slice(r * BF16_ROWS, (r + 1) * BF16_ROWS)
            w[r] = w[r] + jnp.where(r2_ref[h, jr, lanes] < n_b, b_ref[h, jr, lanes], zero) * a_b
    for r in range(groups):
        er = slice(rows.start + r * BF16_ROWS, rows.start + (r + 1) * BF16_ROWS)
        p_ref[er, lanes] = w[r] * _gelu(act_ref[er, lanes]).astype(BF16)


def _expert_kernel(hn_ref, u_ref, vt_ref, n_ref, a_ref, r2_ref, b_ref, x_ref, out_ref, acc_ref, act_ref, p_ref):
    e = pl.program_id(1)

    @pl.when(e == 0)
    def _():
        acc_ref[...] = jnp.zeros_like(acc_ref)

    act_ref[...] = lax.dot_general(u_ref[...], hn_ref[...], _NT, preferred_element_type=F32)
    for k in range((ET_EXPERT // PEER_N_KEYS) * (act_ref.shape[1] // LW_EXPERT)):
        _gate_slab(k, act_ref, p_ref, n_ref, a_ref, r2_ref, b_ref)
    acc_ref[...] += jnp.dot(vt_ref[...], p_ref[...], preferred_element_type=F32)

    @pl.when(e == pl.num_programs(1) - 1)
    def _():
        out_ref[...] = x_ref[...] + acc_ref[...].T


def _expert(hn, u, vt, n_i, a_i, r2, b_j, x):
    t = hn.shape[0]
    n_exp = u.shape[0]
    tt = TT_EXPERT
    et = ET_EXPERT
    rows_per_step = et // PEER_N_KEYS
    tok = lambda i, e: (i, 0)
    per_i = pl.BlockSpec((PEER_HEADS, rows_per_step, tt), lambda i, e: (0, e, i))
    per_j = pl.BlockSpec((PEER_HEADS, PEER_N_KEYS, tt), lambda i, e: (0, 0, i))
    return pl.pallas_call(
        _expert_kernel,
        grid=(t // tt, n_exp // et),
        in_specs=[
            pl.BlockSpec((tt, D_MODEL), tok),
            pl.BlockSpec((et, D_MODEL), lambda i, e: (e, 0)),
            pl.BlockSpec((D_MODEL, et), lambda i, e: (0, e)),
            per_i, per_i, per_j, per_j,
            pl.BlockSpec((tt, D_MODEL), tok),
        ],
        out_specs=pl.BlockSpec((tt, D_MODEL), tok),
        out_shape=jax.ShapeDtypeStruct((t, D_MODEL), F32),
        scratch_shapes=[pltpu.VMEM((D_MODEL, tt), F32), pltpu.VMEM((et, tt), F32), pltpu.VMEM((et, tt), BF16)],
        compiler_params=_params("parallel", "arbitrary"),
        name="expert",
    )(hn, u, vt, n_i, a_i, r2, b_j, x)


def _rope_tables(seq):
    half = HEAD_DIM // 2
    inv = ROPE_THETA ** (-jnp.arange(half, dtype=F32) / half)
    ang = jnp.arange(seq, dtype=jnp.int32).astype(F32)[:, None] * inv[None, :]
    cos = jnp.cos(ang)
    sin = jnp.sin(ang)
    cos_h = jnp.concatenate([cos, cos], axis=-1)
    sin_h = jnp.concatenate([-sin, sin], axis=-1)
    return jnp.tile(cos_h, (1, GQA_Q_HEADS)), jnp.tile(sin_h, (1, GQA_Q_HEADS))


def kernel(x, norm_mix, w_in, gate_bias, qk_norm, na_rpb, swa_sink, w_branch_na, w_branch_swa, w_out, norm_ffn,
           peer_query, peer_sub_keys, peer_down, peer_up):
    batch, seq, d = x.shape
    t = batch * seq
    depth = w_in.shape[0]
    rows = seq // GRID_W
    cos, sin = _rope_tables(seq)
    lane_head = jnp.arange(NA_WIDTH) // HEAD_DIM
    gmat = jnp.where(lane_head[:, None] == lane_head[None, :], 1.0 / HEAD_DIM, 0.0).astype(BF16)
    xf = x.reshape(t, d)
    for l in range(depth):
        qkn = jnp.tile(qk_norm[l], (1, NA_HEADS))
        qa, ka, va, qb, qbs, kb, vb, gates = _inproj(
            xf, norm_mix[l][None, :], w_in[l].astype(BF16), qkn, gate_bias[l].reshape(1, -1), cos, sin, gmat, seq)
        oa = _na_attention(qa, ka, va, _na_bias_tiles(na_rpb[l], rows), batch, seq)
        ob = _swa_attention(swa_sink[l], qb, qbs, kb, vb, batch, seq)
        xf, hn, qp = _merge(xf, oa, ob, gates, w_branch_na[l].astype(BF16), w_branch_swa[l].astype(BF16),
                            w_out[l].astype(BF16), norm_ffn[l][None, :], peer_query[l].astype(BF16))
        n_i, a_i, r2, b_j = _route(qp, peer_sub_keys[l])
        xf = _expert(hn, peer_down[l].astype(BF16), peer_up[l].T.astype(BF16), n_i, a_i, r2, b_j, xf)
    return xf.reshape(batch, seq, d)
```

```python
import functools

import jax
import jax.numpy as jnp
import numpy as np
from jax import lax
from jax.experimental import pallas as pl
from jax.experimental.pallas import tpu as pltpu

F32 = jnp.float32
BF16 = jnp.bfloat16

D_MODEL = 1024
HEAD_DIM = 64
NA_HEADS = 8
GQA_Q_HEADS = 8
GQA_KV_HEADS = 2
GQA_GROUP = GQA_Q_HEADS // GQA_KV_HEADS
GRID_W = 64
NA_WIN_ROWS = 8
NA_WIN_COLS = 16
SWA_WINDOW = 128
SWA_BLOCK = 128
ROPE_THETA = 10000.0
PEER_HEADS = 8
PEER_N_KEYS = 128
PEER_TOPK = 16
PEER_KEY_DIM = 256
EPS = 1e-6

NA_WIDTH = NA_HEADS * HEAD_DIM
GQA_Q_WIDTH = GQA_Q_HEADS * HEAD_DIM
GQA_KV_WIDTH = GQA_KV_HEADS * HEAD_DIM
COL_QA = 0
COL_KA = NA_WIDTH
COL_VA = 2 * NA_WIDTH
COL_QB = 3 * NA_WIDTH
COL_KB = COL_QB + GQA_Q_WIDTH
COL_VB = COL_KB + GQA_KV_WIDTH
COL_GATE = COL_VB + GQA_KV_WIDTH

LANES = 128
NEG_BIG = -1e30
NO_RANK = 99.0
VMEM_LIMIT = 56 * 1024 * 1024

TM_PROJ = 256
TT_ROUTE = 128
HEADS_ROUTE = 4
TT_EXPERT = 512
ET_EXPERT = 2048
LW_EXPERT = 256
BF16_ROWS = 16
F32_ROWS = 8

_NT = (((1,), (1,)), ((), ()))


def _params(*sem):
    return pltpu.CompilerParams(dimension_semantics=sem, vmem_limit_bytes=VMEM_LIMIT)


def _rms(x, gain):
    ms = jnp.mean(x * x, axis=-1, keepdims=True)
    return x * lax.rsqrt(ms + EPS) * gain


def _head_rms(y, gain, gmat):
    y2 = y * y
    hi = y2.astype(BF16)
    lo = (y2 - hi.astype(F32)).astype(BF16)
    ms = jnp.dot(hi, gmat, preferred_element_type=F32) + jnp.dot(lo, gmat, preferred_element_type=F32)
    return y * lax.rsqrt(ms + EPS) * gain


def _rotary(y, cos, sin_signed):
    width = y.shape[-1]
    lane = lax.broadcasted_iota(jnp.int32, (1, width), 1)
    first_half = (lane % HEAD_DIM) < (HEAD_DIM // 2)
    partner = jnp.where(first_half, pltpu.roll(y, width - HEAD_DIM // 2, 1), pltpu.roll(y, HEAD_DIM // 2, 1))
    return y * cos + partner * sin_signed


def _swap_halves(y):
    blocks = [pltpu.roll(y[:, c:c + LANES], HEAD_DIM, 1) for c in range(0, y.shape[-1], LANES)]
    return blocks[0] if len(blocks) == 1 else jnp.concatenate(blocks, axis=-1)


def _inproj_kernel(x_ref, g_ref, w_ref, qkn_ref, gb_ref, cos_ref, sin_ref, gmat_ref,
                   qa_ref, ka_ref, va_ref, qb_ref, qbs_ref, kb_ref, vb_ref, gate_ref):
    h = _rms(x_ref[...], g_ref[...]).astype(BF16)

    def proj(c0, width):
        return jnp.dot(h, w_ref[:, c0:c0 + width], preferred_element_type=F32)

    gmat = gmat_ref[...]
    scale = HEAD_DIM ** -0.5
    qa_ref[...] = (_head_rms(proj(COL_QA, NA_WIDTH), qkn_ref[0:1, :], gmat) * scale).astype(BF16)
    ka_ref[...] = _head_rms(proj(COL_KA, NA_WIDTH), qkn_ref[1:2, :], gmat).astype(BF16)
    va_ref[...] = proj(COL_VA, NA_WIDTH).astype(BF16)
    cos = cos_ref[...]
    sin = sin_ref[...]
    qb = _rotary(_head_rms(proj(COL_QB, GQA_Q_WIDTH), qkn_ref[2:3, :], gmat), cos, sin) * scale
    qb_ref[...] = qb.astype(BF16)
    qbs_ref[...] = _swap_halves(qb).astype(BF16)
    kb = _head_rms(proj(COL_KB, GQA_KV_WIDTH), qkn_ref[3:4, :GQA_KV_WIDTH], gmat[:GQA_KV_WIDTH, :GQA_KV_WIDTH])
    kb_ref[...] = _rotary(kb, cos[:, :GQA_KV_WIDTH], sin[:, :GQA_KV_WIDTH]).astype(BF16)
    vb_ref[...] = proj(COL_VB, GQA_KV_WIDTH).astype(BF16)
    gate_ref[...] = jax.nn.sigmoid(proj(COL_GATE, 2 * D_MODEL) + gb_ref[...])


def _inproj(x, gain, w, qkn, gate_bias, cos, sin, gmat, seq):
    t = x.shape[0]
    tm = TM_PROJ
    pos_blocks = seq // tm
    row = lambda i: (i, 0)
    fixed = lambda i: (0, 0)
    out_w = (NA_WIDTH, NA_WIDTH, NA_WIDTH, GQA_Q_WIDTH, GQA_Q_WIDTH, GQA_KV_WIDTH, GQA_KV_WIDTH)
    out_shape = [jax.ShapeDtypeStruct((t, w_), BF16) for w_ in out_w] + [jax.ShapeDtypeStruct((t, 2 * D_MODEL), F32)]
    out_specs = [pl.BlockSpec((tm, w_), row) for w_ in out_w] + [pl.BlockSpec((tm, 2 * D_MODEL), row)]
    return pl.pallas_call(
        _inproj_kernel,
        grid=(t // tm,),
        in_specs=[
            pl.BlockSpec((tm, D_MODEL), row),
            pl.BlockSpec((1, D_MODEL), fixed),
            pl.BlockSpec(w.shape, fixed),
            pl.BlockSpec(qkn.shape, fixed),
            pl.BlockSpec((1, 2 * D_MODEL), fixed),
            pl.BlockSpec((tm, GQA_Q_WIDTH), lambda i: (i % pos_blocks, 0)),
            pl.BlockSpec((tm, GQA_Q_WIDTH), lambda i: (i % pos_blocks, 0)),
            pl.BlockSpec(gmat.shape, fixed),
        ],
        out_specs=out_specs,
        out_shape=out_shape,
        compiler_params=_params("parallel"),
        name="inproj",
    )(x, gain, w, qkn, gate_bias, cos, sin, gmat)


def _na_kernel(q_ref, k_ref, v_ref, bias_ref, o_ref, *, rows):
    r = pl.program_id(1)
    first_key_row = jnp.clip(r - NA_WIN_ROWS // 2, 0, rows - NA_WIN_ROWS)
    start = pl.multiple_of(first_key_row * GRID_W, GRID_W)
    nkeys = NA_WIN_ROWS * GRID_W
    lane = lax.broadcasted_iota(jnp.int32, (1, LANES), 1)
    low = lane < HEAD_DIM
    scores = []
    for head in range(NA_HEADS):
        cols = slice((head // 2) * LANES, (head // 2 + 1) * LANES)
        q_pair = q_ref[:, cols]
        q_head = jnp.where(low if head % 2 == 0 else jnp.logical_not(low), q_pair, jnp.zeros_like(q_pair))
        k_win = k_ref[pl.ds(start, nkeys), cols]
        scores.append(lax.dot_general(q_head, k_win, _NT, preferred_element_type=F32) + bias_ref[head, 0])
    probs = []
    for s in scores:
        e = jnp.exp(s - jnp.max(s, axis=-1, keepdims=True))
        probs.append((e.astype(BF16), jnp.sum(e, axis=-1, keepdims=True)))
    outs = []
    for head, (e, denom) in enumerate(probs):
        cols = slice((head // 2) * LANES, (head // 2 + 1) * LANES)
        v_win = v_ref[pl.ds(start, nkeys), cols]
        outs.append(jnp.dot(e, v_win, preferred_element_type=F32) / denom)
    for p in range(NA_HEADS // 2):
        o_ref[:, p * LANES:(p + 1) * LANES] = jnp.where(low, outs[2 * p], outs[2 * p + 1]).astype(BF16)


def _bias_class(r, rows):
    half = NA_WIN_ROWS // 2
    return jnp.where(r < half, r, jnp.where(r > rows - half, r - (rows - NA_WIN_ROWS), half))


def _na_attention(q, k, v, bias, batch, seq):
    rows = seq // GRID_W
    return pl.pallas_call(
        functools.partial(_na_kernel, rows=rows),
        grid=(batch, rows),
        in_specs=[
            pl.BlockSpec((GRID_W, NA_WIDTH), lambda b, r: (b * rows + r, 0)),
            pl.BlockSpec((seq, NA_WIDTH), lambda b, r: (b, 0)),
            pl.BlockSpec((seq, NA_WIDTH), lambda b, r: (b, 0)),
            pl.BlockSpec((NA_HEADS, 1, GRID_W, NA_WIN_ROWS * GRID_W), lambda b, r: (0, _bias_class(r, rows), 0, 0)),
        ],
        out_specs=pl.BlockSpec((GRID_W, NA_WIDTH), lambda b, r: (b * rows + r, 0)),
        out_shape=jax.ShapeDtypeStruct(q.shape, BF16),
        compiler_params=_params("parallel", "arbitrary"),
        name="na",
    )(q, k, v, bias)


def _na_bias_tiles(rpb, rows):
    half = NA_WIN_ROWS // 2
    cls_rows = np.array(list(range(half)) + [half] + list(range(rows - half + 1, rows)))
    rs = np.clip(cls_rows - half, 0, rows - NA_WIN_ROWS)
    dr_idx = rs[:, None] + np.arange(NA_WIN_ROWS)[None, :] - cls_rows[:, None] + (NA_WIN_ROWS - 1)
    c = np.arange(GRID_W)
    cs = np.clip(c - NA_WIN_COLS // 2, 0, GRID_W - NA_WIN_COLS)
    dc_idx = np.clip(c[None, :] - c[:, None] + NA_WIN_COLS - 1, 0, 2 * NA_WIN_COLS - 2)
    in_win = (c[None, :] >= cs[:, None]) & (c[None, :] < cs[:, None] + NA_WIN_COLS)
    onehot = (dc_idx[None] == np.arange(2 * NA_WIN_COLS - 1)[:, None, None]).astype(np.float32)
    toep = jnp.einsum("hrc,cqk->hrqk", rpb.astype(F32), onehot, precision=lax.Precision.HIGHEST)
    toep = jnp.where(in_win[None, None], toep, NEG_BIG)
    tiles = jnp.stack([jnp.stack([toep[:, int(dr_idx[k, a])] for a in range(NA_WIN_ROWS)], axis=2)
                       for k in range(len(cls_rows))], axis=1)
    return tiles.reshape(rpb.shape[0], len(cls_rows), GRID_W, NA_WIN_ROWS * GRID_W)


def _swa_kernel(sink_ref, q_ref, qs_ref, k_ref, v_ref, o_ref, *, seq):
    n = pl.program_id(1)
    nkeys = 3 * SWA_BLOCK
    start = pl.multiple_of(jnp.clip((n - 1) * SWA_BLOCK, 0, seq - nkeys), SWA_BLOCK)
    k_win = k_ref[pl.ds(start, nkeys), :]
    v_win = v_ref[pl.ds(start, nkeys), :]
    qpos = n * SWA_BLOCK + lax.broadcasted_iota(jnp.int32, (SWA_BLOCK, nkeys), 0)
    kpos = start + lax.broadcasted_iota(jnp.int32, (SWA_BLOCK, nkeys), 1)
    valid = jnp.abs(qpos - kpos) <= SWA_WINDOW
    lane = lax.broadcasted_iota(jnp.int32, (1, LANES), 1)
    low = lane < HEAD_DIM
    scores = []
    for head in range(GQA_Q_HEADS):
        cols = slice((head // 2) * LANES, (head // 2 + 1) * LANES)
        kv = head // GQA_GROUP
        q_pair = (q_ref if head % 2 == kv else qs_ref)[:, cols]
        q_head = jnp.where(low if kv == 0 else jnp.logical_not(low), q_pair, jnp.zeros_like(q_pair))
        s = lax.dot_general(q_head, k_win, _NT, preferred_element_type=F32)
        scores.append(jnp.where(valid, s, NEG_BIG))
    probs = []
    for head, s in enumerate(scores):
        sink = sink_ref[head]
        m = jnp.maximum(jnp.max(s, axis=-1, keepdims=True), sink)
        e = jnp.exp(s - m)
        probs.append((e.astype(BF16), jnp.sum(e, axis=-1, keepdims=True) + jnp.exp(sink - m)))
    outs = []
    for head, (e, denom) in enumerate(probs):
        o = jnp.dot(e, v_win, preferred_element_type=F32) / denom
        outs.append(o if head % 2 == head // GQA_GROUP else pltpu.roll(o, HEAD_DIM, 1))
    for p in range(GQA_Q_HEADS // 2):
        o_ref[:, p * LANES:(p + 1) * LANES] = jnp.where(low, outs[2 * p], outs[2 * p + 1]).astype(BF16)


def _swa_attention(sink, q, qs, k, v, batch, seq):
    nb = seq // SWA_BLOCK
    return pl.pallas_call(
        functools.partial(_swa_kernel, seq=seq),
        grid=(batch, nb),
        in_specs=[
            pl.BlockSpec(memory_space=pltpu.SMEM),
            pl.BlockSpec((SWA_BLOCK, GQA_Q_WIDTH), lambda b, n: (b * nb + n, 0)),
            pl.BlockSpec((SWA_BLOCK, GQA_Q_WIDTH), lambda b, n: (b * nb + n, 0)),
            pl.BlockSpec((seq, GQA_KV_WIDTH), lambda b, n: (b, 0)),
            pl.BlockSpec((seq, GQA_KV_WIDTH), lambda b, n: (b, 0)),
        ],
        out_specs=pl.BlockSpec((SWA_BLOCK, GQA_Q_WIDTH), lambda b, n: (b * nb + n, 0)),
        out_shape=jax.ShapeDtypeStruct(q.shape, BF16),
        compiler_params=_params("parallel", "arbitrary"),
        name="swa",
    )(sink, q, qs, k, v)


def _merge_kernel(x_ref, oa_ref, ob_ref, gate_ref, wna_ref, wswa_ref, wout_ref, gffn_ref, wq_ref,
                  xnew_ref, hn_ref, qp_ref):
    ya = jnp.dot(oa_ref[...], wna_ref[...], preferred_element_type=F32)
    yb = jnp.dot(ob_ref[...], wswa_ref[...], preferred_element_type=F32)
    merged = gate_ref[:, :D_MODEL] * ya + gate_ref[:, D_MODEL:] * yb
    xn = x_ref[...] + jnp.dot(merged.astype(BF16), wout_ref[...], preferred_element_type=F32)
    xnew_ref[...] = xn
    hn = _rms(xn, gffn_ref[...]).astype(BF16)
    hn_ref[...] = hn
    qp_ref[...] = jnp.dot(hn, wq_ref[...], preferred_element_type=F32)


def _merge(x, oa, ob, gates, wna, wswa, wout, gffn, wq):
    t = x.shape[0]
    tm = TM_PROJ
    row = lambda i: (i, 0)
    fixed = lambda i: (0, 0)
    nq = wq.shape[1]
    return pl.pallas_call(
        _merge_kernel,
        grid=(t // tm,),
        in_specs=[
            pl.BlockSpec((tm, D_MODEL), row),
            pl.BlockSpec((tm, NA_WIDTH), row),
            pl.BlockSpec((tm, GQA_Q_WIDTH), row),
            pl.BlockSpec((tm, 2 * D_MODEL), row),
            pl.BlockSpec(wna.shape, fixed),
            pl.BlockSpec(wswa.shape, fixed),
            pl.BlockSpec(wout.shape, fixed),
            pl.BlockSpec((1, D_MODEL), fixed),
            pl.BlockSpec(wq.shape, fixed),
        ],
        out_specs=[pl.BlockSpec((tm, D_MODEL), row), pl.BlockSpec((tm, D_MODEL), row), pl.BlockSpec((tm, nq), row)],
        out_shape=[jax.ShapeDtypeStruct((t, D_MODEL), F32), jax.ShapeDtypeStruct((t, D_MODEL), BF16),
                   jax.ShapeDtypeStruct((t, nq), F32)],
        compiler_params=_params("parallel"),
        name="merge",
    )(x, oa, ob, gates, wna, wswa, wout, gffn, wq)


def _top16_many(arrays, want_rank):
    arrays = list(arrays)
    ranks = [jnp.full(s.shape, NO_RANK, F32) if want else None for s, want in zip(arrays, want_rank)]
    rows = [[] for _ in arrays]
    for it in range(PEER_TOPK):
        for k, s in enumerate(arrays):
            m = jnp.max(s, axis=0, keepdims=True)
            hit = s == m
            if want_rank[k]:
                ranks[k] = jnp.where(hit, float(it), ranks[k])
            arrays[k] = jnp.where(hit, -jnp.inf, s)
            rows[k].append(m)
    return rows, ranks


def _route_kernel(qp_ref, keys_ref, n_ref, a_ref, r2_ref, b_ref):
    half = PEER_KEY_DIM // 2
    heads = n_ref.shape[0]
    keys = [keys_ref[0].astype(BF16), keys_ref[1].astype(BF16)]
    scores = []
    for h in range(heads):
        for side in range(2):
            q = qp_ref[:, h * PEER_KEY_DIM + side * half:h * PEER_KEY_DIM + (side + 1) * half].astype(BF16)
            scores.append(lax.dot_general(keys[side], q, _NT, preferred_element_type=F32))
    top_rows, ranks = _top16_many(scores, [k % 2 == 1 for k in range(len(scores))])
    tt = scores[0].shape[1]
    sub = lax.broadcasted_iota(jnp.int32, (F32_ROWS, tt), 0)
    first4 = sub < 4

    def rows8(vals):
        out = jnp.zeros((F32_ROWS, tt), F32)
        for r, v in enumerate(vals):
            out = jnp.where(sub == r, v, out)
        return out

    groups = []
    for h in range(heads):
        v1, v2 = top_rows[2 * h], top_rows[2 * h + 1]
        v2lo = rows8(v2[:8])
        v2dup = rows8(v2[:4] + v2[:4])
        groups.append([
            (v1[0] + v2lo, (0, 0)), (v1[0] + rows8(v2[8:]), (0, 0)), (v1[1] + v2lo, (1, 1)), (v1[2] + v2lo, (2, 2)),
            (jnp.where(first4, v1[3], v1[4]) + v2dup, (3, 4)), (jnp.where(first4, v1[5], v1[6]) + v2dup, (5, 6)),
            (v1[7] + v2lo, (7, 7)), (rows8(v1[8:]) + v2[0], None),
        ])
    cands = [[g for g, _ in head_groups] for head_groups in groups]
    taus = [None] * heads
    for it in range(PEER_TOPK):
        for h in range(heads):
            mx = cands[h][0]
            for c in cands[h][1:]:
                mx = jnp.maximum(mx, c)
            taus[h] = jnp.max(mx, axis=0, keepdims=True)
            if it + 1 < PEER_TOPK:
                cands[h] = [jnp.where(c == taus[h], -jnp.inf, c) for c in cands[h]]

    for h in range(heads):
        v1, v2 = top_rows[2 * h], top_rows[2 * h + 1]
        s1, s2 = scores[2 * h], scores[2 * h + 1]
        top = v1[0] + v2[0]
        z = jnp.zeros((1, tt), F32)
        n_a = [jnp.zeros((1, tt), F32) for _ in range(PEER_TOPK)]
        for g, owners in groups[h]:
            sel = g >= taus[h]
            z = z + jnp.sum(jnp.where(sel, jnp.exp(g - top), 0.0), axis=0, keepdims=True)
            ones = jnp.where(sel, 1.0, 0.0)
            if owners is None:
                for r in range(F32_ROWS):
                    n_a[F32_ROWS + r] = ones[r:r + 1, :]
            elif owners[0] == owners[1]:
                n_a[owners[0]] = n_a[owners[0]] + jnp.sum(ones, axis=0, keepdims=True)
            else:
                n_a[owners[0]] = jnp.sum(jnp.where(first4, ones, 0.0), axis=0, keepdims=True)
                n_a[owners[1]] = jnp.sum(jnp.where(first4, 0.0, ones), axis=0, keepdims=True)
        n_i = jnp.zeros(s1.shape, F32)
        for a in range(PEER_TOPK):
            n_i = jnp.where(s1 == v1[a], n_a[a], n_i)
        n_ref[h] = n_i
        a_ref[h] = jnp.exp(s1 - v1[0]) / z
        r2_ref[h] = ranks[2 * h + 1].astype(BF16)
        b_ref[h] = jnp.exp(s2 - v2[0]).astype(BF16)


def _route(qp, sub_keys):
    t = qp.shape[0]
    tt = TT_ROUTE
    out = jax.ShapeDtypeStruct((PEER_HEADS, PEER_N_KEYS, t), F32)
    out16 = jax.ShapeDtypeStruct((PEER_HEADS, PEER_N_KEYS, t), BF16)
    hs = HEADS_ROUTE
    spec = pl.BlockSpec((hs, PEER_N_KEYS, tt), lambda i, h: (h, 0, i))
    return pl.pallas_call(
        _route_kernel,
        grid=(t // tt, PEER_HEADS // hs),
        in_specs=[
            pl.BlockSpec((tt, hs * PEER_KEY_DIM), lambda i, h: (i, h)),
            pl.BlockSpec(sub_keys.shape, lambda i, h: (0, 0, 0)),
        ],
        out_specs=[spec, spec, spec, spec],
        out_shape=[out, out, out16, out16],
        compiler_params=_params("parallel", "arbitrary"),
        name="route",
    )(qp, sub_keys)


def _gelu(x):
    return 0.5 * x * (1.0 + lax.erf(x * (2.0 ** -0.5)))


def _gate_slab(k, act_ref, p_ref, n_ref, a_ref, r2_ref, b_ref):
    slabs_per_row = act_ref.shape[1] // LW_EXPERT
    ii = k // slabs_per_row
    l0 = (k % slabs_per_row) * LW_EXPERT
    rows = slice(ii * PEER_N_KEYS, (ii + 1) * PEER_N_KEYS)
    lanes = slice(l0, l0 + LW_EXPERT)
    zero = jnp.zeros((BF16_ROWS, LW_EXPERT), BF16)
    groups = PEER_N_KEYS // BF16_ROWS
    w = [zero] * groups
    for h in range(PEER_HEADS):
        n_b = jnp.broadcast_to(n_ref[h, ii:ii + 1, lanes], (BF16_ROWS, LW_EXPERT)).astype(BF16)
        a_b = jnp.broadcast_to(a_ref[h, ii:ii + 1, lanes], (BF16_ROWS, LW_EXPERT)).astype(BF16)
        for r in range(groups):
            jr = slice(r * BF16_ROWS, (r + 1) * BF16_ROWS)
            w[r] = w[r] + jnp.where(r2_ref[h, jr, lanes] < n_b, b_ref[h, jr, lanes], zero) * a_b
    for r in range(groups):
        er = slice(rows.start + r * BF16_ROWS, rows.start + (r + 1) * BF16_ROWS)
        p_ref[er, lanes] = w[r] * _gelu(act_ref[er, lanes]).astype(BF16)


def _expert_kernel(hn_ref, u_ref, vt_ref, n_ref, a_ref, r2_ref, b_ref, x_ref, out_ref, acc_ref, act_ref, p_ref):
    e = pl.program_id(1)

    @pl.when(e == 0)
    def _():
        acc_ref[...] = jnp.zeros_like(acc_ref)

    act_ref[...] = lax.dot_general(u_ref[...], hn_ref[...], _NT, preferred_element_type=F32)
    for k in range((ET_EXPERT // PEER_N_KEYS) * (act_ref.shape[1] // LW_EXPERT)):
        _gate_slab(k, act_ref, p_ref, n_ref, a_ref, r2_ref, b_ref)
    acc_ref[...] += jnp.dot(vt_ref[...], p_ref[...], preferred_element_type=F32)

    @pl.when(e == pl.num_programs(1) - 1)
    def _():
        out_ref[...] = x_ref[...] + acc_ref[...].T


def _expert(hn, u, vt, n_i, a_i, r2, b_j, x):
    t = hn.shape[0]
    n_exp = u.shape[0]
    tt = TT_EXPERT
    et = ET_EXPERT
    rows_per_step = et // PEER_N_KEYS
    tok = lambda i, e: (i, 0)
    per_i = pl.BlockSpec((PEER_HEADS, rows_per_step, tt), lambda i, e: (0, e, i))
    per_j = pl.BlockSpec((PEER_HEADS, PEER_N_KEYS, tt), lambda i, e: (0, 0, i))
    return pl.pallas_call(
        _expert_kernel,
        grid=(t // tt, n_exp // et),
        in_specs=[
            pl.BlockSpec((tt, D_MODEL), tok),
            pl.BlockSpec((et, D_MODEL), lambda i, e: (e, 0)),
            pl.BlockSpec((D_MODEL, et), lambda i, e: (0, e)),
            per_i, per_i, per_j, per_j,
            pl.BlockSpec((tt, D_MODEL), tok),
        ],
        out_specs=pl.BlockSpec((tt, D_MODEL), tok),
        out_shape=jax.ShapeDtypeStruct((t, D_MODEL), F32),
        scratch_shapes=[pltpu.VMEM((D_MODEL, tt), F32), pltpu.VMEM((et, tt), F32), pltpu.VMEM((et, tt), BF16)],
        compiler_params=_params("parallel", "arbitrary"),
        name="expert",
    )(hn, u, vt, n_i, a_i, r2, b_j, x)


def _rope_tables(seq):
    half = HEAD_DIM // 2
    inv = ROPE_THETA ** (-jnp.arange(half, dtype=F32) / half)
    ang = jnp.arange(seq, dtype=jnp.int32).astype(F32)[:, None] * inv[None, :]
    cos = jnp.cos(ang)
    sin = jnp.sin(ang)
    cos_h = jnp.concatenate([cos, cos], axis=-1)
    sin_h = jnp.concatenate([-sin, sin], axis=-1)
    return jnp.tile(cos_h, (1, GQA_Q_HEADS)), jnp.tile(sin_h, (1, GQA_Q_HEADS))


def kernel(x, norm_mix, w_in, gate_bias, qk_norm, na_rpb, swa_sink, w_branch_na, w_branch_swa, w_out, norm_ffn,
           peer_query, peer_sub_keys, peer_down, peer_up):
    batch, seq, d = x.shape
    t = batch * seq
    depth = w_in.shape[0]
    rows = seq // GRID_W
    cos, sin = _rope_tables(seq)
    lane_head = jnp.arange(NA_WIDTH) // HEAD_DIM
    gmat = jnp.where(lane_head[:, None] == lane_head[None, :], 1.0 / HEAD_DIM, 0.0).astype(BF16)
    xf = x.reshape(t, d)
    for l in range(depth):
        qkn = jnp.tile(qk_norm[l], (1, NA_HEADS))
        qa, ka, va, qb, qbs, kb, vb, gates = _inproj(
            xf, norm_mix[l][None, :], w_in[l].astype(BF16), qkn, gate_bias[l].reshape(1, -1), cos, sin, gmat, seq)
        oa = _na_attention(qa, ka, va, _na_bias_tiles(na_rpb[l], rows), batch, seq)
        ob = _swa_attention(swa_sink[l], qb, qbs, kb, vb, batch, seq)
        xf, hn, qp = _merge(xf, oa, ob, gates, w_branch_na[l].astype(BF16), w_branch_swa[l].astype(BF16),
                            w_out[l].astype(BF16), norm_ffn[l][None, :], peer_query[l].astype(BF16))
        n_i, a_i, r2, b_j = _route(qp, peer_sub_keys[l])
        xf = _expert(hn, peer_down[l].astype(BF16), peer_up[l].T.astype(BF16), n_i, a_i, r2, b_j, xf)
    return xf.reshape(batch, seq, d)
```

```python
import functools

import jax
import jax.numpy as jnp
import numpy as np
from jax import lax
from jax.experimental import pallas as pl
from jax.experimental.pallas import tpu as pltpu

F32 = jnp.float32
BF16 = jnp.bfloat16

D_MODEL = 1024
HEAD_DIM = 64
NA_HEADS = 8
GQA_Q_HEADS = 8
GQA_KV_HEADS = 2
GQA_GROUP = GQA_Q_HEADS // GQA_KV_HEADS
GRID_W = 64
NA_WIN_ROWS = 8
NA_WIN_COLS = 16
SWA_WINDOW = 128
SWA_BLOCK = 128
ROPE_THETA = 10000.0
PEER_HEADS = 8
PEER_N_KEYS = 128
PEER_TOPK = 16
PEER_KEY_DIM = 256
EPS = 1e-6

NA_WIDTH = NA_HEADS * HEAD_DIM
GQA_Q_WIDTH = GQA_Q_HEADS * HEAD_DIM
GQA_KV_WIDTH = GQA_KV_HEADS * HEAD_DIM
COL_QA = 0
COL_KA = NA_WIDTH
COL_VA = 2 * NA_WIDTH
COL_QB = 3 * NA_WIDTH
COL_KB = COL_QB + GQA_Q_WIDTH
COL_VB = COL_KB + GQA_KV_WIDTH
COL_GATE = COL_VB + GQA_KV_WIDTH

LANES = 128
NEG_BIG = -1e30
NO_RANK = 99.0
VMEM_LIMIT = 56 * 1024 * 1024

NA_ROWS_PER_STEP = 4
SWA_BLOCKS_PER_STEP = 2
TM_PROJ = 256
TT_ROUTE = 128
HEADS_ROUTE = 8
TT_EXPERT = 512
ET_EXPERT = 2048
LW_EXPERT = 256
BF16_ROWS = 16
F32_ROWS = 8

_NT = (((1,), (1,)), ((), ()))


def _params(*sem):
    return pltpu.CompilerParams(dimension_semantics=sem, vmem_limit_bytes=VMEM_LIMIT)


def _rms(x, gain):
    ms = jnp.mean(x * x, axis=-1, keepdims=True)
    return x * lax.rsqrt(ms + EPS) * gain


def _head_rms(y, gain, gmat):
    y2 = y * y
    hi = y2.astype(BF16)
    lo = (y2 - hi.astype(F32)).astype(BF16)
    ms = jnp.dot(hi, gmat, preferred_element_type=F32) + jnp.dot(lo, gmat, preferred_element_type=F32)
    return y * lax.rsqrt(ms + EPS) * gain


def _rotary(y, cos, sin_signed):
    width = y.shape[-1]
    lane = lax.broadcasted_iota(jnp.int32, (1, width), 1)
    first_half = (lane % HEAD_DIM) < (HEAD_DIM // 2)
    partner = jnp.where(first_half, pltpu.roll(y, width - HEAD_DIM // 2, 1), pltpu.roll(y, HEAD_DIM // 2, 1))
    return y * cos + partner * sin_signed


def _swap_halves(y):
    blocks = [pltpu.roll(y[:, c:c + LANES], HEAD_DIM, 1) for c in range(0, y.shape[-1], LANES)]
    return blocks[0] if len(blocks) == 1 else jnp.concatenate(blocks, axis=-1)


def _inproj_kernel(x_ref, g_ref, w_ref, qkn_ref, gb_ref, cos_ref, sin_ref, gmat_ref,
                   qa_ref, ka_ref, va_ref, qb_ref, qbs_ref, kb_ref, vb_ref, gate_ref):
    h = _rms(x_ref[...], g_ref[...]).astype(BF16)

    def proj(c0, width):
        return jnp.dot(h, w_ref[:, c0:c0 + width], preferred_element_type=F32)

    gmat = gmat_ref[...]
    scale = HEAD_DIM ** -0.5
    qa_ref[...] = (_head_rms(proj(COL_QA, NA_WIDTH), qkn_ref[0:1, :], gmat) * scale).astype(BF16)
    ka_ref[...] = _head_rms(proj(COL_KA, NA_WIDTH), qkn_ref[1:2, :], gmat).astype(BF16)
    va_ref[...] = proj(COL_VA, NA_WIDTH).astype(BF16)
    cos = cos_ref[...]
    sin = sin_ref[...]
    qb = _rotary(_head_rms(proj(COL_QB, GQA_Q_WIDTH), qkn_ref[2:3, :], gmat), cos, sin) * scale
    qb_ref[...] = qb.astype(BF16)
    qbs_ref[...] = _swap_halves(qb).astype(BF16)
    kb = _head_rms(proj(COL_KB, GQA_KV_WIDTH), qkn_ref[3:4, :GQA_KV_WIDTH], gmat[:GQA_KV_WIDTH, :GQA_KV_WIDTH])
    kb_ref[...] = _rotary(kb, cos[:, :GQA_KV_WIDTH], sin[:, :GQA_KV_WIDTH]).astype(BF16)
    vb_ref[...] = proj(COL_VB, GQA_KV_WIDTH).astype(BF16)
    gate_ref[...] = jax.nn.sigmoid(proj(COL_GATE, 2 * D_MODEL) + gb_ref[...])


def _inproj(x, gain, w, qkn, gate_bias, cos, sin, gmat, seq):
    t = x.shape[0]
    tm = TM_PROJ
    pos_blocks = seq // tm
    row = lambda i: (i, 0)
    fixed = lambda i: (0, 0)
    out_w = (NA_WIDTH, NA_WIDTH, NA_WIDTH, GQA_Q_WIDTH, GQA_Q_WIDTH, GQA_KV_WIDTH, GQA_KV_WIDTH)
    out_shape = [jax.ShapeDtypeStruct((t, w_), BF16) for w_ in out_w] + [jax.ShapeDtypeStruct((t, 2 * D_MODEL), F32)]
    out_specs = [pl.BlockSpec((tm, w_), row) for w_ in out_w] + [pl.BlockSpec((tm, 2 * D_MODEL), row)]
    return pl.pallas_call(
        _inproj_kernel,
        grid=(t // tm,),
        in_specs=[
            pl.BlockSpec((tm, D_MODEL), row),
            pl.BlockSpec((1, D_MODEL), fixed),
            pl.BlockSpec(w.shape, fixed),
            pl.BlockSpec(qkn.shape, fixed),
            pl.BlockSpec((1, 2 * D_MODEL), fixed),
            pl.BlockSpec((tm, GQA_Q_WIDTH), lambda i: (i % pos_blocks, 0)),
            pl.BlockSpec((tm, GQA_Q_WIDTH), lambda i: (i % pos_blocks, 0)),
            pl.BlockSpec(gmat.shape, fixed),
        ],
        out_specs=out_specs,
        out_shape=out_shape,
        compiler_params=_params("parallel"),
        name="inproj",
    )(x, gain, w, qkn, gate_bias, cos, sin, gmat)


def _na_kernel(q_ref, k_ref, v_ref, *rest, rows):
    bias_refs, o_ref = rest[:-1], rest[-1]
    nkeys = NA_WIN_ROWS * GRID_W
    lane = lax.broadcasted_iota(jnp.int32, (1, LANES), 1)
    low = lane < HEAD_DIM
    starts = []
    for i in range(NA_ROWS_PER_STEP):
        r = pl.program_id(1) * NA_ROWS_PER_STEP + i
        first_key_row = jnp.clip(r - NA_WIN_ROWS // 2, 0, rows - NA_WIN_ROWS)
        starts.append(pl.multiple_of(first_key_row * GRID_W, GRID_W))
    chains = [(i, head) for i in range(NA_ROWS_PER_STEP) for head in range(NA_HEADS)]
    scores = []
    for i, head in chains:
        cols = slice((head // 2) * LANES, (head // 2 + 1) * LANES)
        q_pair = q_ref[i * GRID_W:(i + 1) * GRID_W, cols]
        q_head = jnp.where(low if head % 2 == 0 else jnp.logical_not(low), q_pair, jnp.zeros_like(q_pair))
        k_win = k_ref[pl.ds(starts[i], nkeys), cols]
        scores.append(lax.dot_general(q_head, k_win, _NT, preferred_element_type=F32) + bias_refs[i][head, 0])
    probs = []
    for s in scores:
        e = jnp.exp(s - jnp.max(s, axis=-1, keepdims=True))
        probs.append((e.astype(BF16), jnp.sum(e, axis=-1, keepdims=True)))
    outs = []
    for (i, head), (e, denom) in zip(chains, probs):
        cols = slice((head // 2) * LANES, (head // 2 + 1) * LANES)
        v_win = v_ref[pl.ds(starts[i], nkeys), cols]
        outs.append(jnp.dot(e, v_win, preferred_element_type=F32) / denom)
    for i in range(NA_ROWS_PER_STEP):
        for p in range(NA_HEADS // 2):
            pair = jnp.where(low, outs[i * NA_HEADS + 2 * p], outs[i * NA_HEADS + 2 * p + 1])
            o_ref[i * GRID_W:(i + 1) * GRID_W, p * LANES:(p + 1) * LANES] = pair.astype(BF16)


def _bias_class(r, rows):
    half = NA_WIN_ROWS // 2
    return jnp.where(r < half, r, jnp.where(r > rows - half, r - (rows - NA_WIN_ROWS), half))


def _na_attention(q, k, v, bias, batch, seq):
    rows = seq // GRID_W
    rps = NA_ROWS_PER_STEP
    steps = rows // rps
    q_spec = pl.BlockSpec((rps * GRID_W, NA_WIDTH), lambda b, r: (b * steps + r, 0))

    def bias_spec(i):
        return pl.BlockSpec((NA_HEADS, 1, GRID_W, NA_WIN_ROWS * GRID_W),
                            lambda b, r: (0, _bias_class(r * rps + i, rows), 0, 0))

    return pl.pallas_call(
        functools.partial(_na_kernel, rows=rows),
        grid=(batch, steps),
        in_specs=[
            q_spec,
            pl.BlockSpec((seq, NA_WIDTH), lambda b, r: (b, 0)),
            pl.BlockSpec((seq, NA_WIDTH), lambda b, r: (b, 0)),
        ] + [bias_spec(i) for i in range(rps)],
        out_specs=q_spec,
        out_shape=jax.ShapeDtypeStruct(q.shape, BF16),
        compiler_params=_params("parallel", "arbitrary"),
        name="na",
    )(q, k, v, *([bias] * rps))


def _na_bias_tiles(rpb, rows):
    half = NA_WIN_ROWS // 2
    cls_rows = np.array(list(range(half)) + [half] + list(range(rows - half + 1, rows)))
    rs = np.clip(cls_rows - half, 0, rows - NA_WIN_ROWS)
    dr_idx = rs[:, None] + np.arange(NA_WIN_ROWS)[None, :] - cls_rows[:, None] + (NA_WIN_ROWS - 1)
    c = np.arange(GRID_W)
    cs = np.clip(c - NA_WIN_COLS // 2, 0, GRID_W - NA_WIN_COLS)
    dc_idx = np.clip(c[None, :] - c[:, None] + NA_WIN_COLS - 1, 0, 2 * NA_WIN_COLS - 2)
    in_win = (c[None, :] >= cs[:, None]) & (c[None, :] < cs[:, None] + NA_WIN_COLS)
    onehot = (dc_idx[None] == np.arange(2 * NA_WIN_COLS - 1)[:, None, None]).astype(np.float32)
    toep = jnp.einsum("hrc,cqk->hrqk", rpb.astype(F32), onehot, precision=lax.Precision.HIGHEST)
    toep = jnp.where(in_win[None, None], toep, NEG_BIG)
    tiles = jnp.stack([jnp.stack([toep[:, int(dr_idx[k, a])] for a in range(NA_WIN_ROWS)], axis=2)
                       for k in range(len(cls_rows))], axis=1)
    return tiles.reshape(rpb.shape[0], len(cls_rows), GRID_W, NA_WIN_ROWS * GRID_W)


def _swa_kernel(sink_ref, q_ref, qs_ref, k_ref, v_ref, o_ref, *, seq):
    nkeys = 3 * SWA_BLOCK
    lane = lax.broadcasted_iota(jnp.int32, (1, LANES), 1)
    low = lane < HEAD_DIM
    windows = []
    for i in range(SWA_BLOCKS_PER_STEP):
        n = pl.program_id(1) * SWA_BLOCKS_PER_STEP + i
        start = pl.multiple_of(jnp.clip((n - 1) * SWA_BLOCK, 0, seq - nkeys), SWA_BLOCK)
        qpos = n * SWA_BLOCK + lax.broadcasted_iota(jnp.int32, (SWA_BLOCK, nkeys), 0)
        kpos = start + lax.broadcasted_iota(jnp.int32, (SWA_BLOCK, nkeys), 1)
        windows.append((k_ref[pl.ds(start, nkeys), :], v_ref[pl.ds(start, nkeys), :],
                        jnp.abs(qpos - kpos) <= SWA_WINDOW))
    chains = [(i, head) for i in range(SWA_BLOCKS_PER_STEP) for head in range(GQA_Q_HEADS)]
    scores = []
    for i, head in chains:
        cols = slice((head // 2) * LANES, (head // 2 + 1) * LANES)
        kv = head // GQA_GROUP
        q_pair = (q_ref if head % 2 == kv else qs_ref)[i * SWA_BLOCK:(i + 1) * SWA_BLOCK, cols]
        q_head = jnp.where(low if kv == 0 else jnp.logical_not(low), q_pair, jnp.zeros_like(q_pair))
        s = lax.dot_general(q_head, windows[i][0], _NT, preferred_element_type=F32)
        scores.append(jnp.where(windows[i][2], s, NEG_BIG))
    probs = []
    for (i, head), s in zip(chains, scores):
        sink = sink_ref[head]
        m = jnp.maximum(jnp.max(s, axis=-1, keepdims=True), sink)
        e = jnp.exp(s - m)
        probs.append((e.astype(BF16), jnp.sum(e, axis=-1, keepdims=True) + jnp.exp(sink - m)))
    outs = []
    for (i, head), (e, denom) in zip(chains, probs):
        o = jnp.dot(e, windows[i][1], preferred_element_type=F32) / denom
        outs.append(o if head % 2 == head // GQA_GROUP else pltpu.roll(o, HEAD_DIM, 1))
    for i in range(SWA_BLOCKS_PER_STEP):
        for p in range(GQA_Q_HEADS // 2):
            pair = jnp.where(low, outs[i * GQA_Q_HEADS + 2 * p], outs[i * GQA_Q_HEADS + 2 * p + 1])
            o_ref[i * SWA_BLOCK:(i + 1) * SWA_BLOCK, p * LANES:(p + 1) * LANES] = pair.astype(BF16)


def _swa_attention(sink, q, qs, k, v, batch, seq):
    steps = seq // (SWA_BLOCK * SWA_BLOCKS_PER_STEP)
    q_spec = pl.BlockSpec((SWA_BLOCKS_PER_STEP * SWA_BLOCK, GQA_Q_WIDTH), lambda b, n: (b * steps + n, 0))
    return pl.pallas_call(
        functools.partial(_swa_kernel, seq=seq),
        grid=(batch, steps),
        in_specs=[
            pl.BlockSpec(memory_space=pltpu.SMEM),
            q_spec,
            q_spec,
            pl.BlockSpec((seq, GQA_KV_WIDTH), lambda b, n: (b, 0)),
            pl.BlockSpec((seq, GQA_KV_WIDTH), lambda b, n: (b, 0)),
        ],
        out_specs=q_spec,
        out_shape=jax.ShapeDtypeStruct(q.shape, BF16),
        compiler_params=_params("parallel", "arbitrary"),
        name="swa",
    )(sink, q, qs, k, v)


def _merge_kernel(x_ref, oa_ref, ob_ref, gate_ref, wna_ref, wswa_ref, wout_ref, gffn_ref, wq_ref,
                  xnew_ref, hn_ref, qp_ref):
    ya = jnp.dot(oa_ref[...], wna_ref[...], preferred_element_type=F32)
    yb = jnp.dot(ob_ref[...], wswa_ref[...], preferred_element_type=F32)
    merged = gate_ref[:, :D_MODEL] * ya + gate_ref[:, D_MODEL:] * yb
    xn = x_ref[...] + jnp.dot(merged.astype(BF16), wout_ref[...], preferred_element_type=F32)
    xnew_ref[...] = xn
    hn = _rms(xn, gffn_ref[...]).astype(BF16)
    hn_ref[...] = hn
    qp_ref[...] = jnp.dot(hn, wq_ref[...], preferred_element_type=F32)


def _merge(x, oa, ob, gates, wna, wswa, wout, gffn, wq):
    t = x.shape[0]
    tm = TM_PROJ
    row = lambda i: (i, 0)
    fixed = lambda i: (0, 0)
    nq = wq.shape[1]
    return pl.pallas_call(
        _merge_kernel,
        grid=(t // tm,),
        in_specs=[
            pl.BlockSpec((tm, D_MODEL), row),
            pl.BlockSpec((tm, NA_WIDTH), row),
            pl.BlockSpec((tm, GQA_Q_WIDTH), row),
            pl.BlockSpec((tm, 2 * D_MODEL), row),
            pl.BlockSpec(wna.shape, fixed),
            pl.BlockSpec(wswa.shape, fixed),
            pl.BlockSpec(wout.shape, fixed),
            pl.BlockSpec((1, D_MODEL), fixed),
            pl.BlockSpec(wq.shape, fixed),
        ],
        out_specs=[pl.BlockSpec((tm, D_MODEL), row), pl.BlockSpec((tm, D_MODEL), row), pl.BlockSpec((tm, nq), row)],
        out_shape=[jax.ShapeDtypeStruct((t, D_MODEL), F32), jax.ShapeDtypeStruct((t, D_MODEL), BF16),
                   jax.ShapeDtypeStruct((t, nq), F32)],
        compiler_params=_params("parallel"),
        name="merge",
    )(x, oa, ob, gates, wna, wswa, wout, gffn, wq)


def _top16_many(arrays, want_rank):
    arrays = list(arrays)
    ranks = [jnp.full(s.shape, NO_RANK, F32) if want else None for s, want in zip(arrays, want_rank)]
    rows = [[] for _ in arrays]
    for it in range(PEER_TOPK):
        for k, s in enumerate(arrays):
            m = jnp.max(s, axis=0, keepdims=True)
            hit = s == m
            if want_rank[k]:
                ranks[k] = jnp.where(hit, float(it), ranks[k])
            arrays[k] = jnp.where(hit, -jnp.inf, s)
            rows[k].append(m)
    return rows, ranks


def _route_kernel(qp_ref, keys_ref, n_ref, a_ref, r2_ref, b_ref):
    half = PEER_KEY_DIM // 2
    heads = n_ref.shape[0]
    keys = [keys_ref[0].astype(BF16), keys_ref[1].astype(BF16)]
    scores = []
    for h in range(heads):
        for side in range(2):
            q = qp_ref[:, h * PEER_KEY_DIM + side * half:h * PEER_KEY_DIM + (side + 1) * half].astype(BF16)
            scores.append(lax.dot_general(keys[side], q, _NT, preferred_element_type=F32))
    top_rows, ranks = _top16_many(scores, [k % 2 == 1 for k in range(len(scores))])
    tt = scores[0].shape[1]
    sub = lax.broadcasted_iota(jnp.int32, (F32_ROWS, tt), 0)
    first4 = sub < 4

    def rows8(vals):
        out = jnp.zeros((F32_ROWS, tt), F32)
        for r, v in enumerate(vals):
            out = jnp.where(sub == r, v, out)
        return out

    groups = []
    for h in range(heads):
        v1, v2 = top_rows[2 * h], top_rows[2 * h + 1]
        v2lo = rows8(v2[:8])
        v2dup = rows8(v2[:4] + v2[:4])
        groups.append([
            (v1[0] + v2lo, (0, 0)), (v1[0] + rows8(v2[8:]), (0, 0)), (v1[1] + v2lo, (1, 1)), (v1[2] + v2lo, (2, 2)),
            (jnp.where(first4, v1[3], v1[4]) + v2dup, (3, 4)), (jnp.where(first4, v1[5], v1[6]) + v2dup, (5, 6)),
            (v1[7] + v2lo, (7, 7)), (rows8(v1[8:]) + v2[0], None),
        ])
    cands = [[g for g, _ in head_groups] for head_groups in groups]
    taus = [None] * heads
    for it in range(PEER_TOPK):
        for h in range(heads):
            mx = cands[h][0]
            for c in cands[h][1:]:
                mx = jnp.maximum(mx, c)
            taus[h] = jnp.max(mx, axis=0, keepdims=True)
            if it + 1 < PEER_TOPK:
                cands[h] = [jnp.where(c == taus[h], -jnp.inf, c) for c in cands[h]]

    for h in range(heads):
        v1, v2 = top_rows[2 * h], top_rows[2 * h + 1]
        s1, s2 = scores[2 * h], scores[2 * h + 1]
        top = v1[0] + v2[0]
        z = jnp.zeros((1, tt), F32)
        n_a = [jnp.zeros((1, tt), F32) for _ in range(PEER_TOPK)]
        for g, owners in groups[h]:
            sel = g >= taus[h]
            z = z + jnp.sum(jnp.where(sel, jnp.exp(g - top), 0.0), axis=0, keepdims=True)
            ones = jnp.where(sel, 1.0, 0.0)
            if owners is None:
                for r in range(F32_ROWS):
                    n_a[F32_ROWS + r] = ones[r:r + 1, :]
            elif owners[0] == owners[1]:
                n_a[owners[0]] = n_a[owners[0]] + jnp.sum(ones, axis=0, keepdims=True)
            else:
                n_a[owners[0]] = jnp.sum(jnp.where(first4, ones, 0.0), axis=0, keepdims=True)
                n_a[owners[1]] = jnp.sum(jnp.where(first4, 0.0, ones), axis=0, keepdims=True)
        n_i = jnp.zeros(s1.shape, F32)
        for a in range(PEER_TOPK):
            n_i = jnp.where(s1 == v1[a], n_a[a], n_i)
        n_ref[h] = n_i
        a_ref[h] = jnp.exp(s1 - v1[0]) / z
        r2_ref[h] = ranks[2 * h + 1].astype(BF16)
        b_ref[h] = jnp.exp(s2 - v2[0]).astype(BF16)


def _route(qp, sub_keys):
    t = qp.shape[0]
    tt = TT_ROUTE
    out = jax.ShapeDtypeStruct((PEER_HEADS, PEER_N_KEYS, t), F32)
    out16 = jax.ShapeDtypeStruct((PEER_HEADS, PEER_N_KEYS, t), BF16)
    hs = HEADS_ROUTE
    spec = pl.BlockSpec((hs, PEER_N_KEYS, tt), lambda i, h: (h, 0, i))
    return pl.pallas_call(
        _route_kernel,
        grid=(t // tt, PEER_HEADS // hs),
        in_specs=[
            pl.BlockSpec((tt, hs * PEER_KEY_DIM), lambda i, h: (i, h)),
            pl.BlockSpec(sub_keys.shape, lambda i, h: (0, 0, 0)),
        ],
        out_specs=[spec, spec, spec, spec],
        out_shape=[out, out, out16, out16],
        compiler_params=_params("parallel", "arbitrary"),
        name="route",
    )(qp, sub_keys)


def _gelu(x):
    return 0.5 * x * (1.0 + lax.erf(x * (2.0 ** -0.5)))


def _gate_slab(k, act_ref, p_ref, n_ref, a_ref, r2_ref, b_ref):
    slabs_per_row = act_ref.shape[1] // LW_EXPERT
    ii = k // slabs_per_row
    l0 = (k % slabs_per_row) * LW_EXPERT
    rows = slice(ii * PEER_N_KEYS, (ii + 1) * PEER_N_KEYS)
    lanes = slice(l0, l0 + LW_EXPERT)
    zero = jnp.zeros((BF16_ROWS, LW_EXPERT), BF16)
    groups = PEER_N_KEYS // BF16_ROWS
    w = [zero] * groups
    for h in range(PEER_HEADS):
        n_b = jnp.broadcast_to(n_ref[h, ii:ii + 1, lanes], (BF16_ROWS, LW_EXPERT)).astype(BF16)
        a_b = jnp.broadcast_to(a_ref[h, ii:ii + 1, lanes], (BF16_ROWS, LW_EXPERT)).astype(BF16)
        for r in range(groups):
            jr = slice(r * BF16_ROWS, (r + 1) * BF16_ROWS)
            w[r] = w[r] + jnp.where(r2_ref[h, jr, lanes] < n_b, b_ref[h, jr, lanes], zero) * a_b
    for r in range(groups):
        er = slice(rows.start + r * BF16_ROWS, rows.start + (r + 1) * BF16_ROWS)
        p_ref[er, lanes] = w[r] * _gelu(act_ref[er, lanes]).astype(BF16)


def _expert_kernel(hn_ref, u_ref, vt_ref, n_ref, a_ref, r2_ref, b_ref, x_ref, out_ref, acc_ref, act_ref, p_ref):
    e = pl.program_id(1)

    @pl.when(e == 0)
    def _():
        acc_ref[...] = jnp.zeros_like(acc_ref)

    act_ref[...] = lax.dot_general(u_ref[...], hn_ref[...], _NT, preferred_element_type=F32)
    for k in range((ET_EXPERT // PEER_N_KEYS) * (act_ref.shape[1] // LW_EXPERT)):
        _gate_slab(k, act_ref, p_ref, n_ref, a_ref, r2_ref, b_ref)
    acc_ref[...] += jnp.dot(vt_ref[...], p_ref[...], preferred_element_type=F32)

    @pl.when(e == pl.num_programs(1) - 1)
    def _():
        out_ref[...] = x_ref[...] + acc_ref[...].T


def _expert(hn, u, vt, n_i, a_i, r2, b_j, x):
    t = hn.shape[0]
    n_exp = u.shape[0]
    tt = TT_EXPERT
    et = ET_EXPERT
    rows_per_step = et // PEER_N_KEYS
    tok = lambda i, e: (i, 0)
    per_i = pl.BlockSpec((PEER_HEADS, rows_per_step, tt), lambda i, e: (0, e, i))
    per_j = pl.BlockSpec((PEER_HEADS, PEER_N_KEYS, tt), lambda i, e: (0, 0, i))
    return pl.pallas_call(
        _expert_kernel,
        grid=(t // tt, n_exp // et),
        in_specs=[
            pl.BlockSpec((tt, D_MODEL), tok),
            pl.BlockSpec((et, D_MODEL), lambda i, e: (e, 0)),
            pl.BlockSpec((D_MODEL, et), lambda i, e: (0, e)),
            per_i, per_i, per_j, per_j,
            pl.BlockSpec((tt, D_MODEL), tok),
        ],
        out_specs=pl.BlockSpec((tt, D_MODEL), tok),
        out_shape=jax.ShapeDtypeStruct((t, D_MODEL), F32),
        scratch_shapes=[pltpu.VMEM((D_MODEL, tt), F32), pltpu.VMEM((et, tt), F32), pltpu.VMEM((et, tt), BF16)],
        compiler_params=_params("parallel", "arbitrary"),
        name="expert",
    )(hn, u, vt, n_i, a_i, r2, b_j, x)


def _rope_tables(seq):
    half = HEAD_DIM // 2
    inv = ROPE_THETA ** (-jnp.arange(half, dtype=F32) / half)
    ang = jnp.arange(seq, dtype=jnp.int32).astype(F32)[:, None] * inv[None, :]
    cos = jnp.cos(ang)
    sin = jnp.sin(ang)
    cos_h = jnp.concatenate([cos, cos], axis=-1)
    sin_h = jnp.concatenate([-sin, sin], axis=-1)
    return jnp.tile(cos_h, (1, GQA_Q_HEADS)), jnp.tile(sin_h, (1, GQA_Q_HEADS))


def kernel(x, norm_mix, w_in, gate_bias, qk_norm, na_rpb, swa_sink, w_branch_na, w_branch_swa, w_out, norm_ffn,
           peer_query, peer_sub_keys, peer_down, peer_up):
    batch, seq, d = x.shape
    t = batch * seq
    depth = w_in.shape[0]
    rows = seq // GRID_W
    cos, sin = _rope_tables(seq)
    lane_head = jnp.arange(NA_WIDTH) // HEAD_DIM
    gmat = jnp.where(lane_head[:, None] == lane_head[None, :], 1.0 / HEAD_DIM, 0.0).astype(BF16)
    xf = x.reshape(t, d)
    for l in range(depth):
        qkn = jnp.tile(qk_norm[l], (1, NA_HEADS))
        qa, ka, va, qb, qbs, kb, vb, gates = _inproj(
            xf, norm_mix[l][None, :], w_in[l].astype(BF16), qkn, gate_bias[l].reshape(1, -1), cos, sin, gmat, seq)
        oa = _na_attention(qa, ka, va, _na_bias_tiles(na_rpb[l], rows), batch, seq)
        ob = _swa_attention(swa_sink[l], qb, qbs, kb, vb, batch, seq)
        xf, hn, qp = _merge(xf, oa, ob, gates, w_branch_na[l].astype(BF16), w_branch_swa[l].astype(BF16),
                            w_out[l].astype(BF16), norm_ffn[l][None, :], peer_query[l].astype(BF16))
        n_i, a_i, r2, b_j = _route(qp, peer_sub_keys[l])
        xf = _expert(hn, peer_down[l].astype(BF16), peer_up[l].T.astype(BF16), n_i, a_i, r2, b_j, xf)
    return xf.reshape(batch, seq, d)
```

```python
import functools

import jax
import jax.numpy as jnp
import numpy as np
from jax import lax
from jax.experimental import pallas as pl
from jax.experimental.pallas import tpu as pltpu

F32 = jnp.float32
BF16 = jnp.bfloat16

D_MODEL = 1024
HEAD_DIM = 64
NA_HEADS = 8
GQA_Q_HEADS = 8
GQA_KV_HEADS = 2
GQA_GROUP = GQA_Q_HEADS // GQA_KV_HEADS
GRID_W = 64
NA_WIN_ROWS = 8
NA_WIN_COLS = 16
SWA_WINDOW = 128
SWA_BLOCK = 128
ROPE_THETA = 10000.0
PEER_HEADS = 8
PEER_N_KEYS = 128
PEER_TOPK = 16
PEER_KEY_DIM = 256
EPS = 1e-6

NA_WIDTH = NA_HEADS * HEAD_DIM
GQA_Q_WIDTH = GQA_Q_HEADS * HEAD_DIM
GQA_KV_WIDTH = GQA_KV_HEADS * HEAD_DIM
COL_QA = 0
COL_KA = NA_WIDTH
COL_VA = 2 * NA_WIDTH
COL_QB = 3 * NA_WIDTH
COL_KB = COL_QB + GQA_Q_WIDTH
COL_VB = COL_KB + GQA_KV_WIDTH
COL_GATE = COL_VB + GQA_KV_WIDTH

LANES = 128
NEG_BIG = -1e30
NO_RANK = 99.0
VMEM_LIMIT = 56 * 1024 * 1024

NA_ROWS_PER_STEP = 8
SWA_BLOCKS_PER_STEP = 4
TM_PROJ = 512
TT_ROUTE = 128
HEADS_ROUTE = 8
TT_EXPERT = 512
ET_EXPERT = 2048
LW_EXPERT = 256
BF16_ROWS = 16
F32_ROWS = 8

_NT = (((1,), (1,)), ((), ()))


def _params(*sem):
    return pltpu.CompilerParams(dimension_semantics=sem, vmem_limit_bytes=VMEM_LIMIT)


def _rms(x, gain):
    ms = jnp.mean(x * x, axis=-1, keepdims=True)
    return x * lax.rsqrt(ms + EPS) * gain


def _head_rms(y, gain, gmat):
    y2 = y * y
    hi = y2.astype(BF16)
    lo = (y2 - hi.astype(F32)).astype(BF16)
    ms = jnp.dot(hi, gmat, preferred_element_type=F32) + jnp.dot(lo, gmat, preferred_element_type=F32)
    return y * lax.rsqrt(ms + EPS) * gain


def _rotary(y, cos, sin_signed):
    width = y.shape[-1]
    lane = lax.broadcasted_iota(jnp.int32, (1, width), 1)
    first_half = (lane % HEAD_DIM) < (HEAD_DIM // 2)
    partner = jnp.where(first_half, pltpu.roll(y, width - HEAD_DIM // 2, 1), pltpu.roll(y, HEAD_DIM // 2, 1))
    return y * cos + partner * sin_signed


def _swap_halves(y):
    blocks = [pltpu.roll(y[:, c:c + LANES], HEAD_DIM, 1) for c in range(0, y.shape[-1], LANES)]
    return blocks[0] if len(blocks) == 1 else jnp.concatenate(blocks, axis=-1)


def _inproj_kernel(x_ref, g_ref, w_ref, qkn_ref, gb_ref, cos_ref, sin_ref, gmat_ref,
                   qa_ref, ka_ref, va_ref, qb_ref, qbs_ref, kb_ref, vb_ref, gate_ref):
    h = _rms(x_ref[...], g_ref[...]).astype(BF16)

    def proj(c0, width):
        return jnp.dot(h, w_ref[:, c0:c0 + width], preferred_element_type=F32)

    gmat = gmat_ref[...]
    scale = HEAD_DIM ** -0.5
    qa_ref[...] = (_head_rms(proj(COL_QA, NA_WIDTH), qkn_ref[0:1, :], gmat) * scale).astype(BF16)
    ka_ref[...] = _head_rms(proj(COL_KA, NA_WIDTH), qkn_ref[1:2, :], gmat).astype(BF16)
    va_ref[...] = proj(COL_VA, NA_WIDTH).astype(BF16)
    cos = cos_ref[...]
    sin = sin_ref[...]
    qb = _rotary(_head_rms(proj(COL_QB, GQA_Q_WIDTH), qkn_ref[2:3, :], gmat), cos, sin) * scale
    qb_ref[...] = qb.astype(BF16)
    qbs_ref[...] = _swap_halves(qb).astype(BF16)
    kb = _head_rms(proj(COL_KB, GQA_KV_WIDTH), qkn_ref[3:4, :GQA_KV_WIDTH], gmat[:GQA_KV_WIDTH, :GQA_KV_WIDTH])
    kb_ref[...] = _rotary(kb, cos[:, :GQA_KV_WIDTH], sin[:, :GQA_KV_WIDTH]).astype(BF16)
    vb_ref[...] = proj(COL_VB, GQA_KV_WIDTH).astype(BF16)
    gate_ref[...] = jax.nn.sigmoid(proj(COL_GATE, 2 * D_MODEL) + gb_ref[...])


def _inproj(x, gain, w, qkn, gate_bias, cos, sin, gmat, seq):
    t = x.shape[0]
    tm = TM_PROJ
    pos_blocks = seq // tm
    row = lambda i: (i, 0)
    fixed = lambda i: (0, 0)
    out_w = (NA_WIDTH, NA_WIDTH, NA_WIDTH, GQA_Q_WIDTH, GQA_Q_WIDTH, GQA_KV_WIDTH, GQA_KV_WIDTH)
    out_shape = [jax.ShapeDtypeStruct((t, w_), BF16) for w_ in out_w] + [jax.ShapeDtypeStruct((t, 2 * D_MODEL), F32)]
    out_specs = [pl.BlockSpec((tm, w_), row) for w_ in out_w] + [pl.BlockSpec((tm, 2 * D_MODEL), row)]
    return pl.pallas_call(
        _inproj_kernel,
        grid=(t // tm,),
        in_specs=[
            pl.BlockSpec((tm, D_MODEL), row),
            pl.BlockSpec((1, D_MODEL), fixed),
            pl.BlockSpec(w.shape, fixed),
            pl.BlockSpec(qkn.shape, fixed),
            pl.BlockSpec((1, 2 * D_MODEL), fixed),
            pl.BlockSpec((tm, GQA_Q_WIDTH), lambda i: (i % pos_blocks, 0)),
            pl.BlockSpec((tm, GQA_Q_WIDTH), lambda i: (i % pos_blocks, 0)),
            pl.BlockSpec(gmat.shape, fixed),
        ],
        out_specs=out_specs,
        out_shape=out_shape,
        compiler_params=_params("parallel"),
        name="inproj",
    )(x, gain, w, qkn, gate_bias, cos, sin, gmat)


def _na_kernel(q_ref, k_ref, v_ref, *rest, rows):
    bias_refs, o_ref = rest[:-1], rest[-1]
    nkeys = NA_WIN_ROWS * GRID_W
    lane = lax.broadcasted_iota(jnp.int32, (1, LANES), 1)
    low = lane < HEAD_DIM
    starts = []
    for i in range(NA_ROWS_PER_STEP):
        r = pl.program_id(1) * NA_ROWS_PER_STEP + i
        first_key_row = jnp.clip(r - NA_WIN_ROWS // 2, 0, rows - NA_WIN_ROWS)
        starts.append(pl.multiple_of(first_key_row * GRID_W, GRID_W))
    chains = [(i, head) for i in range(NA_ROWS_PER_STEP) for head in range(NA_HEADS)]
    scores = []
    for i, head in chains:
        cols = slice((head // 2) * LANES, (head // 2 + 1) * LANES)
        q_pair = q_ref[i * GRID_W:(i + 1) * GRID_W, cols]
        q_head = jnp.where(low if head % 2 == 0 else jnp.logical_not(low), q_pair, jnp.zeros_like(q_pair))
        k_win = k_ref[pl.ds(starts[i], nkeys), cols]
        scores.append(lax.dot_general(q_head, k_win, _NT, preferred_element_type=F32) + bias_refs[i][head, 0])
    probs = []
    for s in scores:
        e = jnp.exp(s - jnp.max(s, axis=-1, keepdims=True))
        probs.append((e.astype(BF16), jnp.sum(e, axis=-1, keepdims=True)))
    outs = []
    for (i, head), (e, denom) in zip(chains, probs):
        cols = slice((head // 2) * LANES, (head // 2 + 1) * LANES)
        v_win = v_ref[pl.ds(starts[i], nkeys), cols]
        outs.append(jnp.dot(e, v_win, preferred_element_type=F32) / denom)
    for i in range(NA_ROWS_PER_STEP):
        for p in range(NA_HEADS // 2):
            pair = jnp.where(low, outs[i * NA_HEADS + 2 * p], outs[i * NA_HEADS + 2 * p + 1])
            o_ref[i * GRID_W:(i + 1) * GRID_W, p * LANES:(p + 1) * LANES] = pair.astype(BF16)


def _bias_class(r, rows):
    half = NA_WIN_ROWS // 2
    return jnp.where(r < half, r, jnp.where(r > rows - half, r - (rows - NA_WIN_ROWS), half))


def _na_attention(q, k, v, bias, batch, seq):
    rows = seq // GRID_W
    rps = NA_ROWS_PER_STEP
    steps = rows // rps
    q_spec = pl.BlockSpec((rps * GRID_W, NA_WIDTH), lambda b, r: (b * steps + r, 0))

    def bias_spec(i):
        return pl.BlockSpec((NA_HEADS, 1, GRID_W, NA_WIN_ROWS * GRID_W),
                            lambda b, r: (0, _bias_class(r * rps + i, rows), 0, 0))

    return pl.pallas_call(
        functools.partial(_na_kernel, rows=rows),
        grid=(batch, steps),
        in_specs=[
            q_spec,
            pl.BlockSpec((seq, NA_WIDTH), lambda b, r: (b, 0)),
            pl.BlockSpec((seq, NA_WIDTH), lambda b, r: (b, 0)),
        ] + [bias_spec(i) for i in range(rps)],
        out_specs=q_spec,
        out_shape=jax.ShapeDtypeStruct(q.shape, BF16),
        compiler_params=_params("parallel", "arbitrary"),
        name="na",
    )(q, k, v, *([bias] * rps))


def _na_bias_tiles(rpb, rows):
    half = NA_WIN_ROWS // 2
    cls_rows = np.array(list(range(half)) + [half] + list(range(rows - half + 1, rows)))
    rs = np.clip(cls_rows - half, 0, rows - NA_WIN_ROWS)
    dr_idx = rs[:, None] + np.arange(NA_WIN_ROWS)[None, :] - cls_rows[:, None] + (NA_WIN_ROWS - 1)
    c = np.arange(GRID_W)
    cs = np.clip(c - NA_WIN_COLS // 2, 0, GRID_W - NA_WIN_COLS)
    dc_idx = np.clip(c[None, :] - c[:, None] + NA_WIN_COLS - 1, 0, 2 * NA_WIN_COLS - 2)
    in_win = (c[None, :] >= cs[:, None]) & (c[None, :] < cs[:, None] + NA_WIN_COLS)
    onehot = (dc_idx[None] == np.arange(2 * NA_WIN_COLS - 1)[:, None, None]).astype(np.float32)
    toep = jnp.einsum("hrc,cqk->hrqk", rpb.astype(F32), onehot, precision=lax.Precision.HIGHEST)
    toep = jnp.where(in_win[None, None], toep, NEG_BIG)
    tiles = jnp.stack([jnp.stack([toep[:, int(dr_idx[k, a])] for a in range(NA_WIN_ROWS)], axis=2)
                       for k in range(len(cls_rows))], axis=1)
    return tiles.reshape(rpb.shape[0], len(cls_rows), GRID_W, NA_WIN_ROWS * GRID_W)


def _swa_kernel(sink_ref, q_ref, qs_ref, k_ref, v_ref, o_ref, *, seq):
    nkeys = 3 * SWA_BLOCK
    lane = lax.broadcasted_iota(jnp.int32, (1, LANES), 1)
    low = lane < HEAD_DIM
    windows = []
    for i in range(SWA_BLOCKS_PER_STEP):
        n = pl.program_id(1) * SWA_BLOCKS_PER_STEP + i
        start = pl.multiple_of(jnp.clip((n - 1) * SWA_BLOCK, 0, seq - nkeys), SWA_BLOCK)
        qpos = n * SWA_BLOCK + lax.broadcasted_iota(jnp.int32, (SWA_BLOCK, nkeys), 0)
        kpos = start + lax.broadcasted_iota(jnp.int32, (SWA_BLOCK, nkeys), 1)
        windows.append((k_ref[pl.ds(start, nkeys), :], v_ref[pl.ds(start, nkeys), :],
                        jnp.abs(qpos - kpos) <= SWA_WINDOW))
    chains = [(i, head) for i in range(SWA_BLOCKS_PER_STEP) for head in range(GQA_Q_HEADS)]
    scores = []
    for i, head in chains:
        cols = slice((head // 2) * LANES, (head // 2 + 1) * LANES)
        kv = head // GQA_GROUP
        q_pair = (q_ref if head % 2 == kv else qs_ref)[i * SWA_BLOCK:(i + 1) * SWA_BLOCK, cols]
        q_head = jnp.where(low if kv == 0 else jnp.logical_not(low), q_pair, jnp.zeros_like(q_pair))
        s = lax.dot_general(q_head, windows[i][0], _NT, preferred_element_type=F32)
        scores.append(jnp.where(windows[i][2], s, NEG_BIG))
    probs = []
    for (i, head), s in zip(chains, scores):
        sink = sink_ref[head]
        m = jnp.maximum(jnp.max(s, axis=-1, keepdims=True), sink)
        e = jnp.exp(s - m)
        probs.append((e.astype(BF16), jnp.sum(e, axis=-1, keepdims=True) + jnp.exp(sink - m)))
    outs = []
    for (i, head), (e, denom) in zip(chains, probs):
        o = jnp.dot(e, windows[i][1], preferred_element_type=F32) / denom
        outs.append(o if head % 2 == head // GQA_GROUP else pltpu.roll(o, HEAD_DIM, 1))
    for i in range(SWA_BLOCKS_PER_STEP):
        for p in range(GQA_Q_HEADS // 2):
            pair = jnp.where(low, outs[i * GQA_Q_HEADS + 2 * p], outs[i * GQA_Q_HEADS + 2 * p + 1])
            o_ref[i * SWA_BLOCK:(i + 1) * SWA_BLOCK, p * LANES:(p + 1) * LANES] = pair.astype(BF16)


def _swa_attention(sink, q, qs, k, v, batch, seq):
    steps = seq // (SWA_BLOCK * SWA_BLOCKS_PER_STEP)
    q_spec = pl.BlockSpec((SWA_BLOCKS_PER_STEP * SWA_BLOCK, GQA_Q_WIDTH), lambda b, n: (b * steps + n, 0))
    return pl.pallas_call(
        functools.partial(_swa_kernel, seq=seq),
        grid=(batch, steps),
        in_specs=[
            pl.BlockSpec(memory_space=pltpu.SMEM),
            q_spec,
            q_spec,
            pl.BlockSpec((seq, GQA_KV_WIDTH), lambda b, n: (b, 0)),
            pl.BlockSpec((seq, GQA_KV_WIDTH), lambda b, n: (b, 0)),
        ],
        out_specs=q_spec,
        out_shape=jax.ShapeDtypeStruct(q.shape, BF16),
        compiler_params=_params("parallel", "arbitrary"),
        name="swa",
    )(sink, q, qs, k, v)


def _merge_kernel(x_ref, oa_ref, ob_ref, gate_ref, wna_ref, wswa_ref, wout_ref, gffn_ref, wq_ref,
                  xnew_ref, hn_ref, qp_ref):
    ya = jnp.dot(oa_ref[...], wna_ref[...], preferred_element_type=F32)
    yb = jnp.dot(ob_ref[...], wswa_ref[...], preferred_element_type=F32)
    merged = gate_ref[:, :D_MODEL] * ya + gate_ref[:, D_MODEL:] * yb
    xn = x_ref[...] + jnp.dot(merged.astype(BF16), wout_ref[...], preferred_element_type=F32)
    xnew_ref[...] = xn
    hn = _rms(xn, gffn_ref[...]).astype(BF16)
    hn_ref[...] = hn
    qp_ref[...] = jnp.dot(hn, wq_ref[...], preferred_element_type=F32)


def _merge(x, oa, ob, gates, wna, wswa, wout, gffn, wq):
    t = x.shape[0]
    tm = TM_PROJ
    row = lambda i: (i, 0)
    fixed = lambda i: (0, 0)
    nq = wq.shape[1]
    return pl.pallas_call(
        _merge_kernel,
        grid=(t // tm,),
        in_specs=[
            pl.BlockSpec((tm, D_MODEL), row),
            pl.BlockSpec((tm, NA_WIDTH), row),
            pl.BlockSpec((tm, GQA_Q_WIDTH), row),
            pl.BlockSpec((tm, 2 * D_MODEL), row),
            pl.BlockSpec(wna.shape, fixed),
            pl.BlockSpec(wswa.shape, fixed),
            pl.BlockSpec(wout.shape, fixed),
            pl.BlockSpec((1, D_MODEL), fixed),
            pl.BlockSpec(wq.shape, fixed),
        ],
        out_specs=[pl.BlockSpec((tm, D_MODEL), row), pl.BlockSpec((tm, D_MODEL), row), pl.BlockSpec((tm, nq), row)],
        out_shape=[jax.ShapeDtypeStruct((t, D_MODEL), F32), jax.ShapeDtypeStruct((t, D_MODEL), BF16),
                   jax.ShapeDtypeStruct((t, nq), F32)],
        compiler_params=_params("parallel"),
        name="merge",
    )(x, oa, ob, gates, wna, wswa, wout, gffn, wq)


def _top16_many(arrays, want_rank):
    arrays = list(arrays)
    ranks = [jnp.full(s.shape, NO_RANK, F32) if want else None for s, want in zip(arrays, want_rank)]
    rows = [[] for _ in arrays]
    for it in range(PEER_TOPK):
        for k, s in enumerate(arrays):
            m = jnp.max(s, axis=0, keepdims=True)
            hit = s == m
            if want_rank[k]:
                ranks[k] = jnp.where(hit, float(it), ranks[k])
            arrays[k] = jnp.where(hit, -jnp.inf, s)
            rows[k].append(m)
    return rows, ranks


def _route_kernel(qp_ref, keys_ref, n_ref, a_ref, r2_ref, b_ref):
    half = PEER_KEY_DIM // 2
    heads = n_ref.shape[0]
    keys = [keys_ref[0].astype(BF16), keys_ref[1].astype(BF16)]
    scores = []
    for h in range(heads):
        for side in range(2):
            q = qp_ref[:, h * PEER_KEY_DIM + side * half:h * PEER_KEY_DIM + (side + 1) * half].astype(BF16)
            scores.append(lax.dot_general(keys[side], q, _NT, preferred_element_type=F32))
    top_rows, ranks = _top16_many(scores, [k % 2 == 1 for k in range(len(scores))])
    tt = scores[0].shape[1]
    sub = lax.broadcasted_iota(jnp.int32, (F32_ROWS, tt), 0)
    first4 = sub < 4

    def rows8(vals):
        out = jnp.zeros((F32_ROWS, tt), F32)
        for r, v in enumerate(vals):
            out = jnp.where(sub == r, v, out)
        return out

    groups = []
    for h in range(heads):
        v1, v2 = top_rows[2 * h], top_rows[2 * h + 1]
        v2lo = rows8(v2[:8])
        v2dup = rows8(v2[:4] + v2[:4])
        groups.append([
            (v1[0] + v2lo, (0, 0)), (v1[0] + rows8(v2[8:]), (0, 0)), (v1[1] + v2lo, (1, 1)), (v1[2] + v2lo, (2, 2)),
            (jnp.where(first4, v1[3], v1[4]) + v2dup, (3, 4)), (jnp.where(first4, v1[5], v1[6]) + v2dup, (5, 6)),
            (v1[7] + v2lo, (7, 7)), (rows8(v1[8:]) + v2[0], None),
        ])
    cands = [[g for g, _ in head_groups] for head_groups in groups]
    taus = [None] * heads
    for it in range(PEER_TOPK):
        for h in range(heads):
            mx = cands[h][0]
            for c in cands[h][1:]:
                mx = jnp.maximum(mx, c)
            taus[h] = jnp.max(mx, axis=0, keepdims=True)
            if it + 1 < PEER_TOPK:
                cands[h] = [jnp.where(c == taus[h], -jnp.inf, c) for c in cands[h]]

    for h in range(heads):
        v1, v2 = top_rows[2 * h], top_rows[2 * h + 1]
        s1, s2 = scores[2 * h], scores[2 * h + 1]
        top = v1[0] + v2[0]
        z = jnp.zeros((1, tt), F32)
        n_a = [jnp.zeros((1, tt), F32) for _ in range(PEER_TOPK)]
        for g, owners in groups[h]:
            sel = g >= taus[h]
            z = z + jnp.sum(jnp.where(sel, jnp.exp(g - top), 0.0), axis=0, keepdims=True)
            ones = jnp.where(sel, 1.0, 0.0)
            if owners is None:
                for r in range(F32_ROWS):
                    n_a[F32_ROWS + r] = ones[r:r + 1, :]
            elif owners[0] == owners[1]:
                n_a[owners[0]] = n_a[owners[0]] + jnp.sum(ones, axis=0, keepdims=True)
            else:
                n_a[owners[0]] = jnp.sum(jnp.where(first4, ones, 0.0), axis=0, keepdims=True)
                n_a[owners[1]] = jnp.sum(jnp.where(first4, 0.0, ones), axis=0, keepdims=True)
        n_i = jnp.zeros(s1.shape, F32)
        for a in range(PEER_TOPK):
            n_i = jnp.where(s1 == v1[a], n_a[a], n_i)
        n_ref[h] = n_i
        a_ref[h] = jnp.exp(s1 - v1[0]) / z
        r2_ref[h] = ranks[2 * h + 1].astype(BF16)
        b_ref[h] = jnp.exp(s2 - v2[0]).astype(BF16)


def _route(qp, sub_keys):
    t = qp.shape[0]
    tt = TT_ROUTE
    out = jax.ShapeDtypeStruct((PEER_HEADS, PEER_N_KEYS, t), F32)
    out16 = jax.ShapeDtypeStruct((PEER_HEADS, PEER_N_KEYS, t), BF16)
    hs = HEADS_ROUTE
    spec = pl.BlockSpec((hs, PEER_N_KEYS, tt), lambda i, h: (h, 0, i))
    return pl.pallas_call(
        _route_kernel,
        grid=(t // tt, PEER_HEADS // hs),
        in_specs=[
            pl.BlockSpec((tt, hs * PEER_KEY_DIM), lambda i, h: (i, h)),
            pl.BlockSpec(sub_keys.shape, lambda i, h: (0, 0, 0)),
        ],
        out_specs=[spec, spec, spec, spec],
        out_shape=[out, out, out16, out16],
        compiler_params=_params("parallel", "arbitrary"),
        name="route",
    )(qp, sub_keys)


def _gelu(x):
    return 0.5 * x * (1.0 + lax.erf(x * (2.0 ** -0.5)))


def _gate_slab(k, act_ref, p_ref, n_ref, a_ref, r2_ref, b_ref):
    slabs_per_row = act_ref.shape[1] // LW_EXPERT
    ii = k // slabs_per_row
    l0 = (k % slabs_per_row) * LW_EXPERT
    rows = slice(ii * PEER_N_KEYS, (ii + 1) * PEER_N_KEYS)
    lanes = slice(l0, l0 + LW_EXPERT)
    zero = jnp.zeros((BF16_ROWS, LW_EXPERT), BF16)
    groups = PEER_N_KEYS // BF16_ROWS
    w = [zero] * groups
    for h in range(PEER_HEADS):
        n_b = jnp.broadcast_to(n_ref[h, ii:ii + 1, lanes], (BF16_ROWS, LW_EXPERT)).astype(BF16)
        a_b = jnp.broadcast_to(a_ref[h, ii:ii + 1, lanes], (BF16_ROWS, LW_EXPERT)).astype(BF16)
        for r in range(groups):
            jr = slice(r * BF16_ROWS, (r + 1) * BF16_ROWS)
            w[r] = w[r] + jnp.where(r2_ref[h, jr, lanes] < n_b, b_ref[h, jr, lanes], zero) * a_b
    for r in range(groups):
        er = slice(rows.start + r * BF16_ROWS, rows.start + (r + 1) * BF16_ROWS)
        p_ref[er, lanes] = w[r] * _gelu(act_ref[er, lanes]).astype(BF16)


def _expert_kernel(hn_ref, u_ref, vt_ref, n_ref, a_ref, r2_ref, b_ref, x_ref, out_ref, acc_ref, act_ref, p_ref):
    e = pl.program_id(1)

    @pl.when(e == 0)
    def _():
        acc_ref[...] = jnp.zeros_like(acc_ref)

    act_ref[...] = lax.dot_general(u_ref[...], hn_ref[...], _NT, preferred_element_type=F32)
    for k in range((ET_EXPERT // PEER_N_KEYS) * (act_ref.shape[1] // LW_EXPERT)):
        _gate_slab(k, act_ref, p_ref, n_ref, a_ref, r2_ref, b_ref)
    acc_ref[...] += jnp.dot(vt_ref[...], p_ref[...], preferred_element_type=F32)

    @pl.when(e == pl.num_programs(1) - 1)
    def _():
        out_ref[...] = x_ref[...] + acc_ref[...].T


def _expert(hn, u, vt, n_i, a_i, r2, b_j, x):
    t = hn.shape[0]
    n_exp = u.shape[0]
    tt = TT_EXPERT
    et = ET_EXPERT
    rows_per_step = et // PEER_N_KEYS
    tok = lambda i, e: (i, 0)
    per_i = pl.BlockSpec((PEER_HEADS, rows_per_step, tt), lambda i, e: (0, e, i))
    per_j = pl.BlockSpec((PEER_HEADS, PEER_N_KEYS, tt), lambda i, e: (0, 0, i))
    return pl.pallas_call(
        _expert_kernel,
        grid=(t // tt, n_exp // et),
        in_specs=[
            pl.BlockSpec((tt, D_MODEL), tok),
            pl.BlockSpec((et, D_MODEL), lambda i, e: (e, 0)),
            pl.BlockSpec((D_MODEL, et), lambda i, e: (0, e)),
            per_i, per_i, per_j, per_j,
            pl.BlockSpec((tt, D_MODEL), tok),
        ],
        out_specs=pl.BlockSpec((tt, D_MODEL), tok),
        out_shape=jax.ShapeDtypeStruct((t, D_MODEL), F32),
        scratch_shapes=[pltpu.VMEM((D_MODEL, tt), F32), pltpu.VMEM((et, tt), F32), pltpu.VMEM((et, tt), BF16)],
        compiler_params=_params("parallel", "arbitrary"),
        name="expert",
    )(hn, u, vt, n_i, a_i, r2, b_j, x)


def _rope_tables(seq):
    half = HEAD_DIM // 2
    inv = ROPE_THETA ** (-jnp.arange(half, dtype=F32) / half)
    ang = jnp.arange(seq, dtype=jnp.int32).astype(F32)[:, None] * inv[None, :]
    cos = jnp.cos(ang)
    sin = jnp.sin(ang)
    cos_h = jnp.concatenate([cos, cos], axis=-1)
    sin_h = jnp.concatenate([-sin, sin], axis=-1)
    return jnp.tile(cos_h, (1, GQA_Q_HEADS)), jnp.tile(sin_h, (1, GQA_Q_HEADS))


def kernel(x, norm_mix, w_in, gate_bias, qk_norm, na_rpb, swa_sink, w_branch_na, w_branch_swa, w_out, norm_ffn,
           peer_query, peer_sub_keys, peer_down, peer_up):
    batch, seq, d = x.shape
    t = batch * seq
    depth = w_in.shape[0]
    rows = seq // GRID_W
    cos, sin = _rope_tables(seq)
    lane_head = jnp.arange(NA_WIDTH) // HEAD_DIM
    gmat = jnp.where(lane_head[:, None] == lane_head[None, :], 1.0 / HEAD_DIM, 0.0).astype(BF16)
    xf = x.reshape(t, d)
    for l in range(depth):
        qkn = jnp.tile(qk_norm[l], (1, NA_HEADS))
        qa, ka, va, qb, qbs, kb, vb, gates = _inproj(
            xf, norm_mix[l][None, :], w_in[l].astype(BF16), qkn, gate_bias[l].reshape(1, -1), cos, sin, gmat, seq)
        oa = _na_attention(qa, ka, va, _na_bias_tiles(na_rpb[l], rows), batch, seq)
        ob = _swa_attention(swa_sink[l], qb, qbs, kb, vb, batch, seq)
        xf, hn, qp = _merge(xf, oa, ob, gates, w_branch_na[l].astype(BF16), w_branch_swa[l].astype(BF16),
                            w_out[l].astype(BF16), norm_ffn[l][None, :], peer_query[l].astype(BF16))
        n_i, a_i, r2, b_j = _route(qp, peer_sub_keys[l])
        xf = _expert(hn, peer_down[l].astype(BF16), peer_up[l].T.astype(BF16), n_i, a_i, r2, b_j, xf)
    return xf.reshape(batch, seq, d)
```
